```python
import math
import jax, jax.numpy as jnp
from jax import lax
import numpy as np

D_MODEL = 2048
BATCH = 4
SEQ = 2048
DEPTH = 1

N_MEM = 256
EPS = 1e-6
GLA_HEADS = 4
GLA_DK_TOTAL = D_MODEL // 2
GLA_DV_TOTAL = D_MODEL
GLA_DK = GLA_DK_TOTAL // GLA_HEADS
GLA_DV = GLA_DV_TOTAL // GLA_HEADS
GLA_RANK = 16
GLA_TAU = 16.0
GLA_CHUNK = 64
SB_HEAD_DIM = 128
SB_HEADS = D_MODEL // SB_HEAD_DIM
SB_WIDTH = SB_HEADS * SB_HEAD_DIM
SB_BLOCK = 128
MEM_HEADS = 4
MEM_WIDTH = D_MODEL
MEM_HEAD_DIM = MEM_WIDTH // MEM_HEADS
N_BRANCHES = 3

IN_SPLIT_SIZES = (
    GLA_DK_TOTAL,
    GLA_DK_TOTAL,
    GLA_DV_TOTAL,
    GLA_DV_TOTAL,
    GLA_RANK,
    SB_WIDTH,
    SB_WIDTH,
    SB_WIDTH,
    SB_WIDTH,
    MEM_WIDTH,
    N_BRANCHES * D_MODEL,
)
IN_TOTAL = sum(IN_SPLIT_SIZES)

kernel_name = "hybrid_gla_stickbreak_memory_layer"


def rmsnorm(x, g):
    xf = x.astype(jnp.float32)
    y = xf * lax.rsqrt(jnp.mean(xf * xf, axis=-1, keepdims=True) + EPS)
    return (y * g.astype(jnp.float32)).astype(x.dtype)


def split_heads(t, n_heads):
    b, s, w = t.shape
    return t.reshape(b, s, n_heads, w // n_heads).transpose(0, 2, 1, 3)


def merge_heads(t):
    b, h, s, d = t.shape
    return t.transpose(0, 2, 1, 3).reshape(b, s, h * d)


def gla_chunked(q, k, v, log_a):
    B, H, T, dk = q.shape
    dv = v.shape[-1]
    C = GLA_CHUNK
    N = T // C
    f32 = jnp.float32

    def to_chunks(t):
        return t.astype(f32).reshape(B, H, N, C, t.shape[-1]).transpose(2, 0, 1, 3, 4)

    qc, kc, vc, ac = (to_chunks(t) for t in (q, k, v, log_a))
    causal = jnp.tril(jnp.ones((C, C), dtype=bool))[:, :, None]

    def step(S, inp):
        qi, ki, vi, ai = inp
        b = jnp.cumsum(ai, axis=-2)
        o_inter = jnp.einsum('bhck,bhkv->bhcv', qi * jnp.exp(b), S)
        diff = b[:, :, :, None, :] - b[:, :, None, :, :]
        decay = jnp.exp(jnp.where(causal, diff, -jnp.inf))
        attn = jnp.sum(qi[:, :, :, None, :] * ki[:, :, None, :, :] * decay, axis=-1)
        o_intra = jnp.einsum('bhij,bhjv->bhiv', attn, vi)
        b_last = b[:, :, -1:, :]
        S_new = (jnp.exp(b_last[:, :, 0, :])[..., None] * S
                 + jnp.einsum('bhck,bhcv->bhkv', ki * jnp.exp(b_last - b), vi))
        return S_new, o_inter + o_intra

    S0 = jnp.zeros((B, H, dk, dv), f32)
    _, o = lax.scan(step, S0, (qc, kc, vc, ac))
    return o.transpose(1, 2, 0, 3, 4).reshape(B, H, T, dv).astype(v.dtype)


def stick_breaking(q, k, v):
    B, H, T, d = q.shape
    scale = 1.0 / math.sqrt(d)
    outs = []
    for blk in range(T // SB_BLOCK):
        q0 = blk * SB_BLOCK
        kend = q0 + SB_BLOCK
        qb = q[:, :, q0:kend]
        kb = k[:, :, :kend]
        vb = v[:, :, :kend]
        z = jnp.einsum('bhtd,bhsd->bhts', qb, kb).astype(jnp.float32) * scale
        t_idx = q0 + jnp.arange(SB_BLOCK)[:, None]
        s_idx = jnp.arange(kend)[None, :]
        strict = s_idx < t_idx
        log_1m_beta = jnp.where(strict, jax.nn.log_sigmoid(-z), 0.0)
        after = lax.cumsum(log_1m_beta, axis=3, reverse=True) - log_1m_beta
        log_w = jax.nn.log_sigmoid(z) + after
        w = jnp.where(strict, jnp.exp(log_w), 0.0)
        outs.append(jnp.einsum('bhts,bhsd->bhtd', w.astype(v.dtype), vb))
    return jnp.concatenate(outs, axis=2)


def memory_attention(q, k, v):
    d = q.shape[-1]
    s = jnp.einsum('bhtd,bhmd->bhtm', q, k).astype(jnp.float32) / math.sqrt(d)
    p = jax.nn.softmax(s, axis=-1).astype(v.dtype)
    return jnp.einsum('bhtm,bhmd->bhtd', p, v)


def setup_inputs(seed: int = 0) -> dict:
    key = jax.random.key(seed)
    ks = jax.random.split(key, 16)
    f32 = jnp.float32
    D = D_MODEL
    nrm = lambda k, shape, fan_in: jax.random.normal(k, shape, f32) * (fan_in ** -0.5)
    gain = lambda k, n: 1.0 + 0.02 * jax.random.normal(k, (n,), f32)
    return {
        "x": jax.random.normal(ks[0], (BATCH, SEQ, D), f32),
        "mem": jax.random.normal(ks[1], (BATCH, N_MEM, D), f32),
        "norm_pre_g": gain(ks[2], D),
        "norm_post_g": gain(ks[3], D),
        "norm_mem_g": gain(ks[4], D),
        "w_in": nrm(ks[5], (D, IN_TOTAL), D),
        "gla_a_w2": nrm(ks[6], (GLA_RANK, GLA_DK_TOTAL), GLA_RANK),
        "gla_a_b": 0.1 * jax.random.normal(ks[7], (GLA_DK_TOTAL,), f32),
        "gla_head_norm_g": gain(ks[8], GLA_DV),
        "w_mem_kv": nrm(ks[9], (D, 2 * MEM_WIDTH), D),
        "w_proj_gla": nrm(ks[10], (GLA_DV_TOTAL, D), GLA_DV_TOTAL),
        "w_proj_sb": nrm(ks[11], (SB_WIDTH, D), SB_WIDTH),
        "w_proj_mem": nrm(ks[12], (MEM_WIDTH, D), MEM_WIDTH),
        "w_out": nrm(ks[13], (D, D), D),
    }


def reference(x, mem, norm_pre_g, norm_post_g, norm_mem_g, w_in, gla_a_w2, gla_a_b,
              gla_head_norm_g, w_mem_kv, w_proj_gla, w_proj_sb, w_proj_mem, w_out):
    B, T, D = x.shape
    mem_h = rmsnorm(mem, norm_mem_g)
    mk, mv = jnp.split(mem_h @ w_mem_kv, 2, axis=-1)
    mk, mv = split_heads(mk, MEM_HEADS), split_heads(mv, MEM_HEADS)

    for _ in range(DEPTH):
        h = rmsnorm(x, norm_pre_g)
        proj = h @ w_in
        idx = np.cumsum(IN_SPLIT_SIZES)[:-1].tolist()
        (gq, gk, gv, gg, ga, sq, sk, sv, sg, mq, gates) = jnp.split(proj, idx, axis=-1)

        log_a = jax.nn.log_sigmoid(
            (ga @ gla_a_w2 + gla_a_b).astype(jnp.float32)) / GLA_TAU
        o_a = gla_chunked(split_heads(gq * (GLA_DK ** -0.5), GLA_HEADS),
                          split_heads(gk, GLA_HEADS),
                          split_heads(gv, GLA_HEADS),
                          split_heads(log_a, GLA_HEADS))
        o_a = merge_heads(rmsnorm(o_a, gla_head_norm_g)) * jax.nn.silu(gg)

        o_b = stick_breaking(split_heads(sq, SB_HEADS),
                             split_heads(sk, SB_HEADS),
                             split_heads(sv, SB_HEADS))
        o_b = merge_heads(o_b) * jax.nn.silu(sg)

        o_m = merge_heads(memory_attention(split_heads(mq, MEM_HEADS), mk, mv))

        g_a, g_b, g_m = jnp.split(jax.nn.sigmoid(gates), N_BRANCHES, axis=-1)
        merged = (g_a * (o_a @ w_proj_gla)
                  + g_b * (o_b @ w_proj_sb)
                  + g_m * (o_m @ w_proj_mem))
        y = merged @ w_out
        x = x + rmsnorm(y, norm_post_g)
    return x
```

```python
import functools
import math

import jax
import jax.numpy as jnp
from jax import lax
from jax.experimental import pallas as pl
from jax.experimental.pallas import tpu as pltpu

F32 = jnp.float32
BF16 = jnp.bfloat16

EPS = 1e-6
N_MEM = 256
GLA_HEADS = 4
GLA_RANK = 16
GLA_TAU = 16.0
SB_HEAD_DIM = 128
MEM_HEADS = 4
N_BRANCHES = 3

LANES = 128
VMEM_LIMIT = 56 * 1024 * 1024
SB_LOG_CUTOFF = -90.0


def _params(sem):
    return pltpu.CompilerParams(dimension_semantics=sem, vmem_limit_bytes=VMEM_LIMIT)


def _split_bf16(x):
    hi = x.astype(BF16)
    lo = (x - hi.astype(F32)).astype(BF16)
    return hi, lo


def _dot(a, b):
    return jnp.dot(a, b, preferred_element_type=F32)


def _dot_nt(a, b):
    return lax.dot_general(a, b, (((1,), (1,)), ((), ())), preferred_element_type=F32)


def _dot_tn(a, b):
    return lax.dot_general(a, b, (((0,), (0,)), ((), ())), preferred_element_type=F32)


def _softplus(z):
    return jnp.maximum(z, 0.0) + jnp.log(1.0 + jnp.exp(-jnp.abs(z)))


def _sigmoid(z):
    return 1.0 / (1.0 + jnp.exp(-z))


def _prenorm_kernel(x_ref, g_ref, wga_ref, h_ref, ga_ref):
    x = x_ref[...]
    ms = jnp.mean(x * x, axis=-1, keepdims=True)
    h = (x * lax.rsqrt(ms + EPS) * g_ref[...]).astype(BF16)
    h_ref[...] = h
    ga_ref[...] = _dot(h, wga_ref[...])


def _prenorm(x2d, g, w_ga, tm):
    m, d = x2d.shape
    return pl.pallas_call(
        _prenorm_kernel,
        grid=(m // tm,),
        in_specs=[
            pl.BlockSpec((tm, d), lambda i: (i, 0)),
            pl.BlockSpec((1, d), lambda i: (0, 0)),
            pl.BlockSpec((d, LANES), lambda i: (0, 0)),
        ],
        out_specs=[
            pl.BlockSpec((tm, d), lambda i: (i, 0)),
            pl.BlockSpec((tm, LANES), lambda i: (i, 0)),
        ],
        out_shape=[
            jax.ShapeDtypeStruct((m, d), BF16),
            jax.ShapeDtypeStruct((m, LANES), F32),
        ],
        compiler_params=_params(("parallel",)),
        name="prenorm",
    )(x2d, g.reshape(1, d), w_ga)


def _matmul_kernel(a_ref, w_ref, o_ref):
    o_ref[...] = _dot(a_ref[...], w_ref[...]).astype(o_ref.dtype)


def _matmul(a, w, tm, tn, out_dtype, name):
    m, k = a.shape
    n = w.shape[1]
    return pl.pallas_call(
        _matmul_kernel,
        grid=(n // tn, m // tm),
        in_specs=[
            pl.BlockSpec((tm, k), lambda j, i: (i, 0)),
            pl.BlockSpec((k, tn), lambda j, i: (0, j)),
        ],
        out_specs=pl.BlockSpec((tm, tn), lambda j, i: (i, j)),
        out_shape=jax.ShapeDtypeStruct((m, n), out_dtype),
        compiler_params=_params(("parallel", "parallel")),
        name=name,
    )(a, w)


def _memkv_kernel(m_ref, g_ref, w_ref, o_ref):
    x = m_ref[...]
    ms = jnp.mean(x * x, axis=-1, keepdims=True)
    h = (x * lax.rsqrt(ms + EPS) * g_ref[...]).astype(BF16)
    o_ref[...] = _dot(h, w_ref[...]).astype(o_ref.dtype)


def _memkv(mem2d, g, w, tn):
    m, d = mem2d.shape
    n = w.shape[1]
    return pl.pallas_call(
        _memkv_kernel,
        grid=(n // tn,),
        in_specs=[
            pl.BlockSpec((m, d), lambda j: (0, 0)),
            pl.BlockSpec((1, d), lambda j: (0, 0)),
            pl.BlockSpec((d, tn), lambda j: (0, j)),
        ],
        out_specs=pl.BlockSpec((m, tn), lambda j: (0, j)),
        out_shape=jax.ShapeDtypeStruct((m, n), BF16),
        compiler_params=_params(("parallel",)),
        name="memkv",
    )(mem2d, g.reshape(1, d), w)


def _gla_kernel(q_ref, k_ref, v_ref, gg_ref, ga_ref, w2_ref, ab_ref, hn_ref,
                o_ref, st_ref, *, cs, scale):
    @pl.when(pl.program_id(2) == 0)
    def _():
        st_ref[...] = jnp.zeros_like(st_ref)

    ga_hi, ga_lo = _split_bf16(ga_ref[...])
    w2_hi, w2_lo = _split_bf16(w2_ref[...])
    z = _dot(ga_hi, w2_hi) + _dot(ga_lo, w2_hi) + _dot(ga_hi, w2_lo) + ab_ref[...]
    log_a = -_softplus(-z) * (1.0 / GLA_TAU)

    row = lax.broadcasted_iota(jnp.int32, (cs, cs), 0)
    col = lax.broadcasted_iota(jnp.int32, (cs, cs), 1)
    causal = col <= row
    tri = jnp.where(causal, 1.0, 0.0).astype(BF16)
    la_hi, la_lo = _split_bf16(log_a)
    b = _dot(tri, la_hi) + _dot(tri, la_lo)
    b_last = b[cs - 1:cs, :]
    b_mid = b[cs // 2 - 1:cs // 2, :]

    q = q_ref[...].astype(F32) * scale
    k = k_ref[...].astype(F32)
    v = v_ref[...]
    q_inter = (q * jnp.exp(b)).astype(BF16)
    k_carry = (k * jnp.exp(b_last - b)).astype(BF16)
    q_intra = (q * jnp.exp(b - b_mid)).astype(BF16)
    k_intra = (k * jnp.exp(b_mid - b)).astype(BF16)

    attn = jnp.where(causal, _dot_nt(q_intra, k_intra), 0.0).astype(BF16)
    st = st_ref[...]
    o = _dot_nt(q_inter, st.astype(BF16)) + _dot(attn, v)
    st_ref[...] = st * jnp.exp(b_last) + _dot_tn(v, k_carry)

    ms = jnp.mean(o * o, axis=-1, keepdims=True)
    o = o * lax.rsqrt(ms + EPS) * hn_ref[...]
    gg = gg_ref[...].astype(F32)
    o_ref[...] = (o * (gg * _sigmoid(gg))).astype(o_ref.dtype)


def _gla(proj, ga, w2_pad, a_b, hn_g, *, batch, seq, heads, dk, dv, col_q, col_k,
         col_v, col_g, cs):
    nc = seq // cs
    kern = functools.partial(_gla_kernel, cs=cs, scale=dk ** -0.5)
    row = lambda b, h, c: b * nc + c
    return pl.pallas_call(
        kern,
        grid=(batch, heads, nc),
        in_specs=[
            pl.BlockSpec((cs, dk), lambda b, h, c: (row(b, h, c), col_q // dk + h)),
            pl.BlockSpec((cs, dk), lambda b, h, c: (row(b, h, c), col_k // dk + h)),
            pl.BlockSpec((cs, dv), lambda b, h, c: (row(b, h, c), col_v // dv + h)),
            pl.BlockSpec((cs, dv), lambda b, h, c: (row(b, h, c), col_g // dv + h)),
            pl.BlockSpec((cs, LANES), lambda b, h, c: (row(b, h, c), 0)),
            pl.BlockSpec((LANES, dk), lambda b, h, c: (0, h)),
            pl.BlockSpec((1, dk), lambda b, h, c: (0, h)),
            pl.BlockSpec((1, dv), lambda b, h, c: (0, 0)),
        ],
        out_specs=pl.BlockSpec((cs, dv), lambda b, h, c: (row(b, h, c), h)),
        out_shape=jax.ShapeDtypeStruct((batch * seq, heads * dv), BF16),
        scratch_shapes=[pltpu.VMEM((dv, dk), F32)],
        compiler_params=_params(("parallel", "parallel", "arbitrary")),
        name="gla",
    )(proj, proj, proj, proj, ga, w2_pad, a_b.reshape(1, -1), hn_g.reshape(1, -1))


def _sb_kernel(q_ref, k_ref, v_ref, sg_ref, o_ref, carry_ref, acc_ref, *, tq, scale):
    qi = pl.program_id(2)
    q = q_ref[...]
    half = tq // LANES

    r = lax.broadcasted_iota(jnp.int32, (2 * LANES, 2 * LANES), 0) % LANES
    c = lax.broadcasted_iota(jnp.int32, (2 * LANES, 2 * LANES), 1)
    cum_w = jnp.where((c >= LANES) | (r > c), 1.0, 0.0).astype(BF16)

    carry_ref[...] = jnp.zeros_like(carry_ref)
    acc_ref[...] = jnp.zeros_like(acc_ref)

    def sweep(key_start, masked):
        kblk = k_ref[pl.ds(key_start, tq), :]
        vblk = v_ref[pl.ds(key_start, tq), :]
        z = _dot_nt(q, kblk) * scale
        sp = _softplus(z)
        log_1m_beta = -sp
        if masked:
            t_idx = lax.broadcasted_iota(jnp.int32, (tq, tq), 0)
            s_idx = lax.broadcasted_iota(jnp.int32, (tq, tq), 1)
            strict = s_idx < t_idx
            log_1m_beta = jnp.where(strict, log_1m_beta, 0.0)
        carry = carry_ref[...]
        after = [None] * half
        for g in reversed(range(half)):
            part = log_1m_beta[:, g * LANES:(g + 1) * LANES]
            hi, lo = _split_bf16(part)
            ct = _dot(jnp.concatenate([hi, lo], axis=1), cum_w)
            after[g] = ct[:, :LANES] + carry
            carry = carry + ct[:, LANES:]
        carry_ref[...] = carry
        log_w = (z - sp) + jnp.concatenate(after, axis=1)
        w = jnp.exp(log_w)
        if masked:
            w = jnp.where(strict, w, 0.0)
        acc_ref[...] += _dot(w.astype(BF16), vblk)

    sweep(pl.multiple_of(qi * tq, tq), True)

    def cond(state):
        blk, top = state
        return jnp.logical_and(blk >= 0, top > SB_LOG_CUTOFF)

    def body(state):
        blk, _ = state
        sweep(pl.multiple_of(blk * tq, tq), False)
        return blk - 1, jnp.max(carry_ref[...])

    lax.while_loop(cond, body, (qi - 1, jnp.max(carry_ref[...])))

    sg = sg_ref[...].astype(F32)
    o_ref[...] = (acc_ref[...] * (sg * _sigmoid(sg))).astype(o_ref.dtype)


def _sb(proj, *, batch, seq, heads, d, col_q, col_k, col_v, col_g, tq):
    nq = seq // tq
    kern = functools.partial(_sb_kernel, tq=tq, scale=1.0 / math.sqrt(d))
    return pl.pallas_call(
        kern,
        grid=(batch, heads, nq),
        in_specs=[
            pl.BlockSpec((tq, d), lambda b, h, i: (b * nq + i, col_q // d + h)),
            pl.BlockSpec((seq, d), lambda b, h, i: (b, col_k // d + h)),
            pl.BlockSpec((seq, d), lambda b, h, i: (b, col_v // d + h)),
            pl.BlockSpec((tq, d), lambda b, h, i: (b * nq + i, col_g // d + h)),
        ],
        out_specs=pl.BlockSpec((tq, d), lambda b, h, i: (b * nq + i, h)),
        out_shape=jax.ShapeDtypeStruct((batch * seq, heads * d), BF16),
        scratch_shapes=[pltpu.VMEM((tq, LANES), F32), pltpu.VMEM((tq, d), F32)],
        compiler_params=_params(("parallel", "parallel", "arbitrary")),
        name="stickbreak",
    )(proj, proj, proj, proj)


def _memattn_kernel(q_ref, k_ref, v_ref, o_ref, *, scale):
    s = _dot_nt(q_ref[...], k_ref[...]) * scale
    s = s - jnp.max(s, axis=-1, keepdims=True)
    p = jnp.exp(s)
    p = p / jnp.sum(p, axis=-1, keepdims=True)
    o_ref[...] = _dot(p.astype(BF16), v_ref[...]).astype(o_ref.dtype)


def _memattn(proj, mkv, *, batch, seq, heads, d, col_q, tq):
    nq = seq // tq
    kern = functools.partial(_memattn_kernel, scale=1.0 / math.sqrt(d))
    return pl.pallas_call(
        kern,
        grid=(batch, heads, nq),
        in_specs=[
            pl.BlockSpec((tq, d), lambda b, h, i: (b * nq + i, col_q // d + h)),
            pl.BlockSpec((N_MEM, d), lambda b, h, i: (b, h)),
            pl.BlockSpec((N_MEM, d), lambda b, h, i: (b, heads + h)),
        ],
        out_specs=pl.BlockSpec((tq, d), lambda b, h, i: (b * nq + i, h)),
        out_shape=jax.ShapeDtypeStruct((batch * seq, heads * d), BF16),
        compiler_params=_params(("parallel", "parallel", "parallel")),
        name="memattn",
    )(proj, mkv, mkv)


def _merge_kernel(oa_ref, ob_ref, om_ref, wa_ref, wb_ref, wm_ref,
                  ga_ref, gb_ref, gm_ref, o_ref):
    acc = _sigmoid(ga_ref[...].astype(F32)) * _dot(oa_ref[...], wa_ref[...])
    acc += _sigmoid(gb_ref[...].astype(F32)) * _dot(ob_ref[...], wb_ref[...])
    acc += _sigmoid(gm_ref[...].astype(F32)) * _dot(om_ref[...], wm_ref[...])
    o_ref[...] = acc.astype(o_ref.dtype)


def _merge(o_a, o_b, o_m, w_a, w_b, w_m, proj, *, col_gates, tm, tn):
    m, d = o_a.shape
    gate_blk = col_gates // tn
    act = lambda: pl.BlockSpec((tm, d), lambda i, j: (i, 0))
    wgt = lambda: pl.BlockSpec((d, tn), lambda i, j: (0, j))
    gate = lambda n: pl.BlockSpec((tm, tn), lambda i, j: (i, gate_blk + n * (d // tn) + j))
    return pl.pallas_call(
        _merge_kernel,
        grid=(m // tm, d // tn),
        in_specs=[act(), act(), act(), wgt(), wgt(), wgt(), gate(0), gate(1), gate(2)],
        out_specs=pl.BlockSpec((tm, tn), lambda i, j: (i, j)),
        out_shape=jax.ShapeDtypeStruct((m, d), BF16),
        compiler_params=_params(("parallel", "parallel")),
        name="merge",
    )(o_a, o_b, o_m, w_a, w_b, w_m, proj, proj, proj)


def _out_kernel(m_ref, w_ref, g_ref, x_ref, o_ref):
    y = _dot(m_ref[...], w_ref[...])
    ms = jnp.mean(y * y, axis=-1, keepdims=True)
    o_ref[...] = x_ref[...] + y * lax.rsqrt(ms + EPS) * g_ref[...]


def _out(merged, w_out, g, x2d, tm):
    m, d = x2d.shape
    return pl.pallas_call(
        _out_kernel,
        grid=(m // tm,),
        in_specs=[
            pl.BlockSpec((tm, d), lambda i: (i, 0)),
            pl.BlockSpec((d, d), lambda i: (0, 0)),
            pl.BlockSpec((1, d), lambda i: (0, 0)),
            pl.BlockSpec((tm, d), lambda i: (i, 0)),
        ],
        out_specs=pl.BlockSpec((tm, d), lambda i: (i, 0)),
        out_shape=jax.ShapeDtypeStruct((m, d), F32),
        compiler_params=_params(("parallel",)),
        name="outproj",
    )(merged, w_out, g.reshape(1, d), x2d)


def kernel(x, mem, norm_pre_g, norm_post_g, norm_mem_g, w_in, gla_a_w2, gla_a_b,
           gla_head_norm_g, w_mem_kv, w_proj_gla, w_proj_sb, w_proj_mem, w_out):
    batch, seq, d = x.shape
    dk_total, dv_total = d // 2, d
    dk, dv = dk_total // GLA_HEADS, dv_total // GLA_HEADS
    sb_heads = d // SB_HEAD_DIM
    mem_d = d // MEM_HEADS

    c_ga = 2 * dk_total + 2 * dv_total
    col = {}
    off = 0
    for name, width in (("gq", dk_total), ("gk", dk_total), ("gv", dv_total), ("gg", dv_total),
                        ("sq", d), ("sk", d), ("sv", d), ("sg", d), ("mq", d),
                        ("gates", N_BRANCHES * d)):
        col[name] = off
        off += width
    w_main = jnp.concatenate([w_in[:, :c_ga], w_in[:, c_ga + GLA_RANK:]], axis=1).astype(BF16)
    w_ga = jnp.pad(w_in[:, c_ga:c_ga + GLA_RANK], ((0, 0), (0, LANES - GLA_RANK))).astype(BF16)
    w2_pad = jnp.pad(gla_a_w2, ((0, LANES - GLA_RANK), (0, 0)))

    x2d = x.reshape(batch * seq, d)
    mem2d = mem.reshape(batch * N_MEM, d)

    mkv = _memkv(mem2d, norm_mem_g, w_mem_kv.astype(BF16), tn=512)
    h, ga = _prenorm(x2d, norm_pre_g, w_ga, tm=512)
    proj = _matmul(h, w_main, tm=1024, tn=1024, out_dtype=BF16, name="inproj")

    o_a = _gla(proj, ga, w2_pad, gla_a_b, gla_head_norm_g, batch=batch, seq=seq,
               heads=GLA_HEADS, dk=dk, dv=dv, col_q=col["gq"], col_k=col["gk"],
               col_v=col["gv"], col_g=col["gg"], cs=128)
    o_b = _sb(proj, batch=batch, seq=seq, heads=sb_heads, d=SB_HEAD_DIM,
              col_q=col["sq"], col_k=col["sk"], col_v=col["sv"], col_g=col["sg"], tq=256)
    o_m = _memattn(proj, mkv, batch=batch, seq=seq, heads=MEM_HEADS, d=mem_d,
                   col_q=col["mq"], tq=512)

    merged = _merge(o_a, o_b, o_m, w_proj_gla.astype(BF16), w_proj_sb.astype(BF16),
                    w_proj_mem.astype(BF16), proj, col_gates=col["gates"], tm=512, tn=512)
    out = _out(merged, w_out.astype(BF16), norm_post_g, x2d, tm=512)
    return out.reshape(batch, seq, d)
```

```python
import functools
import math

import jax
import jax.numpy as jnp
from jax import lax
from jax.experimental import pallas as pl
from jax.experimental.pallas import tpu as pltpu

F32 = jnp.float32
BF16 = jnp.bfloat16

EPS = 1e-6
N_MEM = 256
GLA_HEADS = 4
GLA_RANK = 16
GLA_TAU = 16.0
SB_HEAD_DIM = 128
MEM_HEADS = 4
N_BRANCHES = 3

LANES = 128
VMEM_LIMIT = 56 * 1024 * 1024
SB_LOG2_CUTOFF = 90.0 * math.log2(math.e)


def _params(sem):
    return pltpu.CompilerParams(dimension_semantics=sem, vmem_limit_bytes=VMEM_LIMIT)


def _split_bf16(x):
    hi = x.astype(BF16)
    lo = (x - hi.astype(F32)).astype(BF16)
    return hi, lo


def _dot(a, b):
    return jnp.dot(a, b, preferred_element_type=F32)


def _dot_nt(a, b):
    return lax.dot_general(a, b, (((1,), (1,)), ((), ())), preferred_element_type=F32)


def _dot_tn(a, b):
    return lax.dot_general(a, b, (((0,), (0,)), ((), ())), preferred_element_type=F32)


def _softplus(z):
    return jnp.maximum(z, 0.0) + jnp.log(1.0 + jnp.exp(-jnp.abs(z)))


def _sigmoid(z):
    return 1.0 / (1.0 + jnp.exp(-z))


def _prenorm_kernel(x_ref, g_ref, wga_ref, h_ref, ga_ref):
    x = x_ref[...]
    ms = jnp.mean(x * x, axis=-1, keepdims=True)
    h = (x * lax.rsqrt(ms + EPS) * g_ref[...]).astype(BF16)
    h_ref[...] = h
    ga_ref[...] = _dot(h, wga_ref[...])


def _prenorm(x2d, g, w_ga, tm):
    m, d = x2d.shape
    return pl.pallas_call(
        _prenorm_kernel,
        grid=(m // tm,),
        in_specs=[
            pl.BlockSpec((tm, d), lambda i: (i, 0)),
            pl.BlockSpec((1, d), lambda i: (0, 0)),
            pl.BlockSpec((d, LANES), lambda i: (0, 0)),
        ],
        out_specs=[
            pl.BlockSpec((tm, d), lambda i: (i, 0)),
            pl.BlockSpec((tm, LANES), lambda i: (i, 0)),
        ],
        out_shape=[
            jax.ShapeDtypeStruct((m, d), BF16),
            jax.ShapeDtypeStruct((m, LANES), F32),
        ],
        compiler_params=_params(("parallel",)),
        name="prenorm",
    )(x2d, g.reshape(1, d), w_ga)


CAST_ROWS = 256


def _inproj_kernel(*refs, shift):
    if shift:
        a_ref, w_ref, tail_ref, o_ref, wb_ref = refs
    else:
        a_ref, w_ref, o_ref, wb_ref = refs
    k, tn = wb_ref.shape

    @pl.when(pl.program_id(1) == 0)
    def _():
        for r in range(0, k, CAST_ROWS):
            rows = slice(r, r + CAST_ROWS)
            if shift:
                wide = jnp.concatenate([w_ref[rows, :], tail_ref[rows, :]], axis=1)
                wb_ref[rows, :] = wide[:, shift:shift + tn].astype(BF16)
            else:
                wb_ref[rows, :] = w_ref[rows, :].astype(BF16)

    o_ref[...] = _dot(a_ref[...], wb_ref[...]).astype(o_ref.dtype)


def _inproj(a, w, *, col0, width, shift, tm, tn, name):
    m, k = a.shape
    blk0 = col0 // tn
    in_specs = [
        pl.BlockSpec((tm, k), lambda j, i: (i, 0)),
        pl.BlockSpec((k, tn), lambda j, i: (0, blk0 + j)),
    ]
    args = [a, w]
    if shift:
        tail0 = (col0 + tn) // LANES
        in_specs.append(pl.BlockSpec((k, LANES), lambda j, i: (0, tail0 + j * (tn // LANES))))
        args.append(w)
    return pl.pallas_call(
        functools.partial(_inproj_kernel, shift=shift),
        grid=(width // tn, m // tm),
        in_specs=in_specs,
        out_specs=pl.BlockSpec((tm, tn), lambda j, i: (i, j)),
        out_shape=jax.ShapeDtypeStruct((m, width), BF16),
        scratch_shapes=[pltpu.VMEM((k, tn), BF16)],
        compiler_params=_params(("parallel", "arbitrary")),
        name=name,
    )(*args)


def _memkv_kernel(m_ref, g_ref, w_ref, o_ref):
    x = m_ref[...]
    ms = jnp.mean(x * x, axis=-1, keepdims=True)
    h = (x * lax.rsqrt(ms + EPS) * g_ref[...]).astype(BF16)
    o_ref[...] = _dot(h, w_ref[...].astype(BF16)).astype(o_ref.dtype)


def _memkv(mem2d, g, w, tn):
    m, d = mem2d.shape
    n = w.shape[1]
    return pl.pallas_call(
        _memkv_kernel,
        grid=(n // tn,),
        in_specs=[
            pl.BlockSpec((m, d), lambda j: (0, 0)),
            pl.BlockSpec((1, d), lambda j: (0, 0)),
            pl.BlockSpec((d, tn), lambda j: (0, j)),
        ],
        out_specs=pl.BlockSpec((m, tn), lambda j: (0, j)),
        out_shape=jax.ShapeDtypeStruct((m, n), BF16),
        compiler_params=_params(("parallel",)),
        name="memkv",
    )(mem2d, g.reshape(1, d), w)


def _gla_kernel(q_ref, k_ref, v_ref, gg_ref, ga_ref, w2h_ref, w2l_ref, ab_ref, hn_ref,
                o_ref, st_ref, *, cs, nh, dk, dv, scale):
    @pl.when(pl.program_id(2) == 0)
    def _():
        st_ref[...] = jnp.zeros_like(st_ref)

    ga_hi, ga_lo = _split_bf16(ga_ref[...])
    w2_hi = w2h_ref[...]
    z = _dot(ga_hi, w2_hi) + _dot(ga_lo, w2_hi) + _dot(ga_hi, w2l_ref[...]) + ab_ref[...]
    log_a = -_softplus(-z) * (1.0 / GLA_TAU)

    row = lax.broadcasted_iota(jnp.int32, (cs, cs), 0)
    col = lax.broadcasted_iota(jnp.int32, (cs, cs), 1)
    causal = col <= row
    tri = jnp.where(causal, 1.0, 0.0).astype(BF16)
    la_hi, la_lo = _split_bf16(log_a)
    b_all = _dot(tri, la_hi) + _dot(tri, la_lo)

    for h in range(nh):
        kc = slice(h * dk, (h + 1) * dk)
        vc = slice(h * dv, (h + 1) * dv)
        b = b_all[:, kc]
        b_last = b[cs - 1:cs, :]
        b_mid = b[cs // 2 - 1:cs // 2, :]
        q = q_ref[:, kc].astype(F32) * scale
        k = k_ref[:, kc].astype(F32)
        v = v_ref[:, vc]
        q_inter = (q * jnp.exp(b)).astype(BF16)
        k_carry = (k * jnp.exp(b_last - b)).astype(BF16)
        q_intra = (q * jnp.exp(b - b_mid)).astype(BF16)
        k_intra = (k * jnp.exp(b_mid - b)).astype(BF16)

        attn = jnp.where(causal, _dot_nt(q_intra, k_intra), 0.0).astype(BF16)
        st = st_ref[h]
        o = _dot_nt(q_inter, st.astype(BF16)) + _dot(attn, v)
        st_ref[h] = st * jnp.exp(b_last) + _dot_tn(v, k_carry)

        ms = jnp.mean(o * o, axis=-1, keepdims=True)
        o = o * lax.rsqrt(ms + EPS) * hn_ref[...]
        gg = gg_ref[:, vc].astype(F32)
        o_ref[:, vc] = (o * (gg * _sigmoid(gg))).astype(o_ref.dtype)


def _gla(proj, ga, w2, a_b, hn_g, *, batch, seq, heads, dk, dv, col_q, col_k,
         col_v, col_g, cs, nh):
    nc = seq // cs
    wk, wv = nh * dk, nh * dv
    w2_hi, w2_lo = _split_bf16(jnp.pad(w2, ((0, LANES - w2.shape[0]), (0, 0))))
    kern = functools.partial(_gla_kernel, cs=cs, nh=nh, dk=dk, dv=dv, scale=dk ** -0.5)
    row = lambda b, c: b * nc + c
    return pl.pallas_call(
        kern,
        grid=(batch, heads // nh, nc),
        in_specs=[
            pl.BlockSpec((cs, wk), lambda b, h, c: (row(b, c), col_q // wk + h)),
            pl.BlockSpec((cs, wk), lambda b, h, c: (row(b, c), col_k // wk + h)),
            pl.BlockSpec((cs, wv), lambda b, h, c: (row(b, c), col_v // wv + h)),
            pl.BlockSpec((cs, wv), lambda b, h, c: (row(b, c), col_g // wv + h)),
            pl.BlockSpec((cs, LANES), lambda b, h, c: (row(b, c), 0)),
            pl.BlockSpec((LANES, wk), lambda b, h, c: (0, h)),
            pl.BlockSpec((LANES, wk), lambda b, h, c: (0, h)),
            pl.BlockSpec((1, wk), lambda b, h, c: (0, h)),
            pl.BlockSpec((1, dv), lambda b, h, c: (0, 0)),
        ],
        out_specs=pl.BlockSpec((cs, wv), lambda b, h, c: (row(b, c), h)),
        out_shape=jax.ShapeDtypeStruct((batch * seq, heads * dv), BF16),
        scratch_shapes=[pltpu.VMEM((nh, dv, dk), F32)],
        compiler_params=_params(("parallel", "parallel", "arbitrary")),
        name="gla",
    )(proj, proj, proj, proj, ga, w2_hi, w2_lo, a_b.reshape(1, -1), hn_g.reshape(1, -1))


def _sb_kernel(q_ref, k_ref, v_ref, sg_ref, o_ref, carry_ref, acc_ref, *, tq, nh, d, scale2):
    qi = pl.program_id(2)
    groups = tq // LANES

    r = lax.broadcasted_iota(jnp.int32, (2 * LANES, 2 * LANES), 0) % LANES
    c = lax.broadcasted_iota(jnp.int32, (2 * LANES, 2 * LANES), 1)
    cum_w = jnp.where((c >= LANES) | (r > c), 1.0, 0.0).astype(BF16)

    carry_ref[...] = jnp.zeros_like(carry_ref)
    acc_ref[...] = jnp.zeros_like(acc_ref)

    def sweep(key_start, masked):
        if masked:
            t_idx = lax.broadcasted_iota(jnp.int32, (tq, tq), 0)
            s_idx = lax.broadcasted_iota(jnp.int32, (tq, tq), 1)
            strict = s_idx < t_idx
        for h in range(nh):
            cols = slice(h * d, (h + 1) * d)
            kblk = k_ref[pl.ds(key_start, tq), cols]
            vblk = v_ref[pl.ds(key_start, tq), cols]
            u = _dot_nt(q_ref[:, cols], kblk) * scale2
            neg_abs = lax.bitcast_convert_type(
                lax.bitcast_convert_type(u, jnp.uint32) | jnp.uint32(0x80000000), F32)
            sp = jnp.maximum(u, 0.0) + jnp.log2(1.0 + jnp.exp2(neg_abs))
            sp_m = jnp.where(strict, sp, 0.0) if masked else sp
            carry = carry_ref[h]
            after = [None] * groups
            for g in reversed(range(groups)):
                hi, lo = _split_bf16(sp_m[:, g * LANES:(g + 1) * LANES])
                ct = _dot(jnp.concatenate([hi, lo], axis=1), cum_w)
                after[g] = ct[:, :LANES] + carry
                carry = carry + ct[:, LANES:]
            carry_ref[h] = carry
            w = jnp.exp2(u - sp - jnp.concatenate(after, axis=1))
            if masked:
                w = jnp.where(strict, w, 0.0)
            acc_ref[h] += _dot(w.astype(BF16), vblk)

    def least_carry():
        m = carry_ref[0]
        for h in range(1, nh):
            m = jnp.minimum(m, carry_ref[h])
        return jnp.min(m)

    sweep(pl.multiple_of(qi * tq, tq), True)

    def cond(state):
        blk, low = state
        return jnp.logical_and(blk >= 0, low < SB_LOG2_CUTOFF)

    def body(state):
        blk, _ = state
        sweep(pl.multiple_of(blk * tq, tq), False)
        return blk - 1, least_carry()

    lax.while_loop(cond, body, (qi - 1, least_carry()))

    for h in range(nh):
        cols = slice(h * d, (h + 1) * d)
        sg = sg_ref[:, cols].astype(F32)
        o_ref[:, cols] = (acc_ref[h] * (sg * _sigmoid(sg))).astype(o_ref.dtype)


def _sb(proj, *, batch, seq, heads, d, col_q, col_k, col_v, col_g, tq, nh):
    nq = seq // tq
    w = nh * d
    kern = functools.partial(_sb_kernel, tq=tq, nh=nh, d=d, scale2=math.log2(math.e) / math.sqrt(d))
    return pl.pallas_call(
        kern,
        grid=(batch, heads // nh, nq),
        in_specs=[
            pl.BlockSpec((tq, w), lambda b, h, i: (b * nq + i, col_q // w + h)),
            pl.BlockSpec((seq, w), lambda b, h, i: (b, col_k // w + h)),
            pl.BlockSpec((seq, w), lambda b, h, i: (b, col_v // w + h)),
            pl.BlockSpec((tq, w), lambda b, h, i: (b * nq + i, col_g // w + h)),
        ],
        out_specs=pl.BlockSpec((tq, w), lambda b, h, i: (b * nq + i, h)),
        out_shape=jax.ShapeDtypeStruct((batch * seq, heads * d), BF16),
        scratch_shapes=[pltpu.VMEM((nh, tq, LANES), F32), pltpu.VMEM((nh, tq, d), F32)],
        compiler_params=_params(("parallel", "parallel", "arbitrary")),
        name="stickbreak",
    )(proj, proj, proj, proj)


def _memattn_kernel(q_ref, k_ref, v_ref, o_ref, *, nh, d, scale):
    for h in range(nh):
        cols = slice(h * d, (h + 1) * d)
        s = _dot_nt(q_ref[:, cols], k_ref[:, cols]) * scale
        s = s - jnp.max(s, axis=-1, keepdims=True)
        p = jnp.exp(s)
        p = p / jnp.sum(p, axis=-1, keepdims=True)
        o_ref[:, cols] = _dot(p.astype(BF16), v_ref[:, cols]).astype(o_ref.dtype)


def _memattn(proj, mkv, *, batch, seq, heads, d, col_q, tq, nh):
    nq = seq // tq
    w = nh * d
    kern = functools.partial(_memattn_kernel, nh=nh, d=d, scale=1.0 / math.sqrt(d))
    return pl.pallas_call(
        kern,
        grid=(batch, heads // nh, nq),
        in_specs=[
            pl.BlockSpec((tq, w), lambda b, h, i: (b * nq + i, col_q // w + h)),
            pl.BlockSpec((N_MEM, w), lambda b, h, i: (b, h)),
            pl.BlockSpec((N_MEM, w), lambda b, h, i: (b, heads // nh + h)),
        ],
        out_specs=pl.BlockSpec((tq, w), lambda b, h, i: (b * nq + i, h)),
        out_shape=jax.ShapeDtypeStruct((batch * seq, heads * d), BF16),
        compiler_params=_params(("parallel", "parallel", "parallel")),
        name="memattn",
    )(proj, mkv, mkv)


def _merge_kernel(oa_ref, ob_ref, om_ref, wa_ref, wb_ref, wm_ref,
                  ga_ref, gb_ref, gm_ref, o_ref):
    acc = _sigmoid(ga_ref[...].astype(F32)) * _dot(oa_ref[...], wa_ref[...])
    acc += _sigmoid(gb_ref[...].astype(F32)) * _dot(ob_ref[...], wb_ref[...])
    acc += _sigmoid(gm_ref[...].astype(F32)) * _dot(om_ref[...], wm_ref[...])
    o_ref[...] = acc.astype(o_ref.dtype)


def _merge(o_a, o_b, o_m, w_a, w_b, w_m, proj, *, col_gates, tm, tn):
    m, d = o_a.shape
    gate_blk = col_gates // tn
    act = lambda: pl.BlockSpec((tm, d), lambda i, j: (i, 0))
    wgt = lambda: pl.BlockSpec((d, tn), lambda i, j: (0, j))
    gate = lambda n: pl.BlockSpec((tm, tn), lambda i, j: (i, gate_blk + n * (d // tn) + j))
    return pl.pallas_call(
        _merge_kernel,
        grid=(m // tm, d // tn),
        in_specs=[act(), act(), act(), wgt(), wgt(), wgt(), gate(0), gate(1), gate(2)],
        out_specs=pl.BlockSpec((tm, tn), lambda i, j: (i, j)),
        out_shape=jax.ShapeDtypeStruct((m, d), BF16),
        compiler_params=_params(("parallel", "parallel")),
        name="merge",
    )(o_a, o_b, o_m, w_a, w_b, w_m, proj, proj, proj)


def _out_kernel(m_ref, w_ref, g_ref, x_ref, o_ref):
    y = _dot(m_ref[...], w_ref[...])
    ms = jnp.mean(y * y, axis=-1, keepdims=True)
    o_ref[...] = x_ref[...] + y * lax.rsqrt(ms + EPS) * g_ref[...]


def _out(merged, w_out, g, x2d, tm):
    m, d = x2d.shape
    return pl.pallas_call(
        _out_kernel,
        grid=(m // tm,),
        in_specs=[
            pl.BlockSpec((tm, d), lambda i: (i, 0)),
            pl.BlockSpec((d, d), lambda i: (0, 0)),
            pl.BlockSpec((1, d), lambda i: (0, 0)),
            pl.BlockSpec((tm, d), lambda i: (i, 0)),
        ],
        out_specs=pl.BlockSpec((tm, d), lambda i: (i, 0)),
        out_shape=jax.ShapeDtypeStruct((m, d), F32),
        compiler_params=_params(("parallel",)),
        name="outproj",
    )(merged, w_out, g.reshape(1, d), x2d)


def kernel(x, mem, norm_pre_g, norm_post_g, norm_mem_g, w_in, gla_a_w2, gla_a_b,
           gla_head_norm_g, w_mem_kv, w_proj_gla, w_proj_sb, w_proj_mem, w_out):
    batch, seq, d = x.shape
    dk_total, dv_total = d // 2, d
    dk, dv = dk_total // GLA_HEADS, dv_total // GLA_HEADS
    sb_heads = d // SB_HEAD_DIM
    mem_d = d // MEM_HEADS

    c_ga = 2 * dk_total + 2 * dv_total
    w_ga = jnp.pad(w_in[:, c_ga:c_ga + GLA_RANK], ((0, 0), (0, LANES - GLA_RANK))).astype(BF16)

    x2d = x.reshape(batch * seq, d)
    mem2d = mem.reshape(batch * N_MEM, d)

    mkv = _memkv(mem2d, norm_mem_g, w_mem_kv, tn=512)
    h, ga = _prenorm(x2d, norm_pre_g, w_ga, tm=512)
    proj_a = _inproj(h, w_in, col0=0, width=c_ga, shift=0, tm=1024, tn=1024, name="inproj_a")
    proj_b = _inproj(h, w_in, col0=c_ga, width=w_in.shape[1] - c_ga - GLA_RANK, shift=GLA_RANK,
                     tm=1024, tn=1024, name="inproj_b")

    o_a = _gla(proj_a, ga, gla_a_w2, gla_a_b, gla_head_norm_g, batch=batch, seq=seq,
               heads=GLA_HEADS, dk=dk, dv=dv, col_q=0, col_k=dk_total, col_v=2 * dk_total,
               col_g=2 * dk_total + dv_total, cs=128, nh=4)
    o_b = _sb(proj_b, batch=batch, seq=seq, heads=sb_heads, d=SB_HEAD_DIM,
              col_q=0, col_k=d, col_v=2 * d, col_g=3 * d, tq=256, nh=4)
    o_m = _memattn(proj_b, mkv, batch=batch, seq=seq, heads=MEM_HEADS, d=mem_d,
                   col_q=4 * d, tq=256, nh=4)

    merged = _merge(o_a, o_b, o_m, w_proj_gla.astype(BF16), w_proj_sb.astype(BF16),
                    w_proj_mem.astype(BF16), proj_b, col_gates=5 * d, tm=512, tn=512)
    out = _out(merged, w_out.astype(BF16), norm_post_g, x2d, tm=512)
    return out.reshape(batch, seq, d)
```

```python
import functools
import math

import jax
import jax.numpy as jnp
from jax import lax
from jax.experimental import pallas as pl
from jax.experimental.pallas import tpu as pltpu

F32 = jnp.float32
BF16 = jnp.bfloat16

EPS = 1e-6
N_MEM = 256
GLA_HEADS = 4
GLA_RANK = 16
GLA_TAU = 16.0
SB_HEAD_DIM = 128
MEM_HEADS = 4
N_BRANCHES = 3

LANES = 128
SUBLANES = 8
VMEM_LIMIT = 56 * 1024 * 1024
SB_LOG2_CUTOFF = 90.0 * math.log2(math.e)


def _params(sem):
    return pltpu.CompilerParams(dimension_semantics=sem, vmem_limit_bytes=VMEM_LIMIT)


def _split_bf16(x):
    hi = x.astype(BF16)
    lo = (x - hi.astype(F32)).astype(BF16)
    return hi, lo


def _dot(a, b):
    return jnp.dot(a, b, preferred_element_type=F32)


def _dot_nt(a, b):
    return lax.dot_general(a, b, (((1,), (1,)), ((), ())), preferred_element_type=F32)


def _dot_tn(a, b):
    return lax.dot_general(a, b, (((0,), (0,)), ((), ())), preferred_element_type=F32)


def _softplus(z):
    return jnp.maximum(z, 0.0) + jnp.log(1.0 + jnp.exp(-jnp.abs(z)))


def _sigmoid(z):
    return 1.0 / (1.0 + jnp.exp(-z))


def _prenorm_kernel(x_ref, g_ref, wga_ref, h_ref, ga_ref):
    x = x_ref[...]
    ms = jnp.mean(x * x, axis=-1, keepdims=True)
    h = (x * lax.rsqrt(ms + EPS) * g_ref[...]).astype(BF16)
    h_ref[...] = h
    ga_ref[...] = _dot_nt(h, wga_ref[...].astype(BF16))


def _prenorm(x2d, g, w_in_t, col_ga, tm):
    m, d = x2d.shape
    return pl.pallas_call(
        _prenorm_kernel,
        grid=(m // tm,),
        in_specs=[
            pl.BlockSpec((tm, d), lambda i: (i, 0)),
            pl.BlockSpec((1, d), lambda i: (0, 0)),
            pl.BlockSpec((LANES, d), lambda i: (col_ga // LANES, 0)),
        ],
        out_specs=[
            pl.BlockSpec((tm, d), lambda i: (i, 0)),
            pl.BlockSpec((tm, LANES), lambda i: (i, 0)),
        ],
        out_shape=[
            jax.ShapeDtypeStruct((m, d), BF16),
            jax.ShapeDtypeStruct((m, LANES), F32),
        ],
        compiler_params=_params(("parallel",)),
        name="prenorm",
    )(x2d, g.reshape(1, d), w_in_t)


CAST_ROWS = 128


def _inproj_kernel(a_ref, w_ref, o_ref, wb_ref):
    @pl.when(pl.program_id(1) == 0)
    def _():
        for r in range(0, wb_ref.shape[0], CAST_ROWS):
            rows = slice(r, r + CAST_ROWS)
            wb_ref[rows, :] = w_ref[rows, :].astype(BF16)

    o_ref[...] = _dot_nt(a_ref[...], wb_ref[...]).astype(o_ref.dtype)


def _inproj(a, w_t, *, row0, width, tm, tn, name):
    m, k = a.shape
    return pl.pallas_call(
        _inproj_kernel,
        grid=(width // tn, m // tm),
        in_specs=[
            pl.BlockSpec((tm, k), lambda j, i: (i, 0)),
            pl.BlockSpec((pl.Element(tn), pl.Element(k)),
                         lambda j, i: ((row0 // SUBLANES + j * (tn // SUBLANES)) * SUBLANES, 0)),
        ],
        out_specs=pl.BlockSpec((tm, tn), lambda j, i: (i, j)),
        out_shape=jax.ShapeDtypeStruct((m, width), BF16),
        scratch_shapes=[pltpu.VMEM((tn, k), BF16)],
        compiler_params=_params(("parallel", "arbitrary")),
        name=name,
    )(a, w_t)


def _memkv_kernel(m_ref, g_ref, w_ref, o_ref):
    x = m_ref[...]
    ms = jnp.mean(x * x, axis=-1, keepdims=True)
    h = (x * lax.rsqrt(ms + EPS) * g_ref[...]).astype(BF16)
    o_ref[...] = _dot(h, w_ref[...].astype(BF16)).astype(o_ref.dtype)


def _memkv(mem2d, g, w, tn):
    m, d = mem2d.shape
    n = w.shape[1]
    return pl.pallas_call(
        _memkv_kernel,
        grid=(n // tn,),
        in_specs=[
            pl.BlockSpec((m, d), lambda j: (0, 0)),
            pl.BlockSpec((1, d), lambda j: (0, 0)),
            pl.BlockSpec((d, tn), lambda j: (0, j)),
        ],
        out_specs=pl.BlockSpec((m, tn), lambda j: (0, j)),
        out_shape=jax.ShapeDtypeStruct((m, n), BF16),
        compiler_params=_params(("parallel",)),
        name="memkv",
    )(mem2d, g.reshape(1, d), w)


def _gla_kernel(q_ref, k_ref, v_ref, gg_ref, ga_ref, w2h_ref, w2l_ref, ab_ref, hn_ref,
                o_ref, st_ref, *, cs, nh, dk, dv, scale):
    @pl.when(pl.program_id(2) == 0)
    def _():
        st_ref[...] = jnp.zeros_like(st_ref)

    ga_hi, ga_lo = _split_bf16(ga_ref[...])
    w2_hi = w2h_ref[...]
    z = _dot(ga_hi, w2_hi) + _dot(ga_lo, w2_hi) + _dot(ga_hi, w2l_ref[...]) + ab_ref[...]
    log_a = -_softplus(-z) * (1.0 / GLA_TAU)

    row = lax.broadcasted_iota(jnp.int32, (cs, cs), 0)
    col = lax.broadcasted_iota(jnp.int32, (cs, cs), 1)
    causal = col <= row
    tri = jnp.where(causal, 1.0, 0.0).astype(BF16)
    la_hi, la_lo = _split_bf16(log_a)
    b_all = _dot(tri, la_hi) + _dot(tri, la_lo)

    for h in range(nh):
        kc = slice(h * dk, (h + 1) * dk)
        vc = slice(h * dv, (h + 1) * dv)
        b = b_all[:, kc]
        b_last = b[cs - 1:cs, :]
        b_mid = b[cs // 2 - 1:cs // 2, :]
        q = q_ref[:, kc].astype(F32) * scale
        k = k_ref[:, kc].astype(F32)
        v = v_ref[:, vc]
        q_inter = (q * jnp.exp(b)).astype(BF16)
        k_carry = (k * jnp.exp(b_last - b)).astype(BF16)
        q_intra = (q * jnp.exp(b - b_mid)).astype(BF16)
        k_intra = (k * jnp.exp(b_mid - b)).astype(BF16)

        attn = jnp.where(causal, _dot_nt(q_intra, k_intra), 0.0).astype(BF16)
        st = st_ref[h]
        o = _dot_nt(q_inter, st.astype(BF16)) + _dot(attn, v)
        st_ref[h] = st * jnp.exp(b_last) + _dot_tn(v, k_carry)

        ms = jnp.mean(o * o, axis=-1, keepdims=True)
        o = o * lax.rsqrt(ms + EPS) * hn_ref[...]
        gg = gg_ref[:, vc].astype(F32)
        o_ref[:, vc] = (o * (gg * _sigmoid(gg))).astype(o_ref.dtype)


def _gla(proj, ga, w2, a_b, hn_g, *, batch, seq, heads, dk, dv, col_q, col_k,
         col_v, col_g, cs, nh):
    nc = seq // cs
    wk, wv = nh * dk, nh * dv
    w2_hi, w2_lo = _split_bf16(jnp.pad(w2, ((0, LANES - w2.shape[0]), (0, 0))))
    kern = functools.partial(_gla_kernel, cs=cs, nh=nh, dk=dk, dv=dv, scale=dk ** -0.5)
    row = lambda b, c: b * nc + c
    return pl.pallas_call(
        kern,
        grid=(batch, heads // nh, nc),
        in_specs=[
            pl.BlockSpec((cs, wk), lambda b, h, c: (row(b, c), col_q // wk + h)),
            pl.BlockSpec((cs, wk), lambda b, h, c: (row(b, c), col_k // wk + h)),
            pl.BlockSpec((cs, wv), lambda b, h, c: (row(b, c), col_v // wv + h)),
            pl.BlockSpec((cs, wv), lambda b, h, c: (row(b, c), col_g // wv + h)),
            pl.BlockSpec((cs, LANES), lambda b, h, c: (row(b, c), 0)),
            pl.BlockSpec((LANES, wk), lambda b, h, c: (0, h)),
            pl.BlockSpec((LANES, wk), lambda b, h, c: (0, h)),
            pl.BlockSpec((1, wk), lambda b, h, c: (0, h)),
            pl.BlockSpec((1, dv), lambda b, h, c: (0, 0)),
        ],
        out_specs=pl.BlockSpec((cs, wv), lambda b, h, c: (row(b, c), h)),
        out_shape=jax.ShapeDtypeStruct((batch * seq, heads * dv), BF16),
        scratch_shapes=[pltpu.VMEM((nh, dv, dk), F32)],
        compiler_params=_params(("parallel", "parallel", "arbitrary")),
        name="gla",
    )(proj, proj, proj, proj, ga, w2_hi, w2_lo, a_b.reshape(1, -1), hn_g.reshape(1, -1))


def _sb_kernel(q_ref, k_ref, v_ref, sg_ref, o_ref, carry_ref, acc_ref, *, tq, nh, d, scale2):
    qi = pl.program_id(2)
    groups = tq // LANES

    r = lax.broadcasted_iota(jnp.int32, (2 * LANES, 2 * LANES), 0) % LANES
    c = lax.broadcasted_iota(jnp.int32, (2 * LANES, 2 * LANES), 1)
    cum_w = jnp.where((c >= LANES) | (r > c), 1.0, 0.0).astype(BF16)

    carry_ref[...] = jnp.zeros_like(carry_ref)
    acc_ref[...] = jnp.zeros_like(acc_ref)

    def sweep(key_start, masked):
        if masked:
            t_idx = lax.broadcasted_iota(jnp.int32, (tq, tq), 0)
            s_idx = lax.broadcasted_iota(jnp.int32, (tq, tq), 1)
            strict = s_idx < t_idx
        for h in range(nh):
            cols = slice(h * d, (h + 1) * d)
            kblk = k_ref[pl.ds(key_start, tq), cols]
            vblk = v_ref[pl.ds(key_start, tq), cols]
            u = _dot_nt(q_ref[:, cols], kblk) * scale2
            sp = jnp.maximum(u, 0.0) + jnp.log2(1.0 + jnp.exp2(-jnp.abs(u)))
            sp_m = jnp.where(strict, sp, 0.0) if masked else sp
            carry = carry_ref[h]
            after = [None] * groups
            for g in reversed(range(groups)):
                hi, lo = _split_bf16(sp_m[:, g * LANES:(g + 1) * LANES])
                ct = _dot(jnp.concatenate([hi, lo], axis=1), cum_w)
                after[g] = ct[:, :LANES] + carry
                carry = carry + ct[:, LANES:]
            carry_ref[h] = carry
            w = jnp.exp2(u - sp - jnp.concatenate(after, axis=1))
            if masked:
                w = jnp.where(strict, w, 0.0)
            acc_ref[h] += _dot(w.astype(BF16), vblk)

    def least_carry():
        m = carry_ref[0]
        for h in range(1, nh):
            m = jnp.minimum(m, carry_ref[h])
        return jnp.min(m)

    sweep(pl.multiple_of(qi * tq, tq), True)

    def cond(state):
        blk, low = state
        return jnp.logical_and(blk >= 0, low < SB_LOG2_CUTOFF)

    def body(state):
        blk, _ = state
        sweep(pl.multiple_of(blk * tq, tq), False)
        return blk - 1, least_carry()

    lax.while_loop(cond, body, (qi - 1, least_carry()))

    for h in range(nh):
        cols = slice(h * d, (h + 1) * d)
        sg = sg_ref[:, cols].astype(F32)
        o_ref[:, cols] = (acc_ref[h] * (sg * _sigmoid(sg))).astype(o_ref.dtype)


def _sb(proj, *, batch, seq, heads, d, col_q, col_k, col_v, col_g, tq, nh):
    nq = seq // tq
    w = nh * d
    kern = functools.partial(_sb_kernel, tq=tq, nh=nh, d=d, scale2=math.log2(math.e) / math.sqrt(d))
    return pl.pallas_call(
        kern,
        grid=(batch, heads // nh, nq),
        in_specs=[
            pl.BlockSpec((tq, w), lambda b, h, i: (b * nq + i, col_q // w + h)),
            pl.BlockSpec((seq, w), lambda b, h, i: (b, col_k // w + h)),
            pl.BlockSpec((seq, w), lambda b, h, i: (b, col_v // w + h)),
            pl.BlockSpec((tq, w), lambda b, h, i: (b * nq + i, col_g // w + h)),
        ],
        out_specs=pl.BlockSpec((tq, w), lambda b, h, i: (b * nq + i, h)),
        out_shape=jax.ShapeDtypeStruct((batch * seq, heads * d), BF16),
        scratch_shapes=[pltpu.VMEM((nh, tq, LANES), F32), pltpu.VMEM((nh, tq, d), F32)],
        compiler_params=_params(("parallel", "parallel", "arbitrary")),
        name="stickbreak",
    )(proj, proj, proj, proj)


def _memattn_kernel(q_ref, k_ref, v_ref, o_ref, *, nh, d, scale):
    for h in range(nh):
        cols = slice(h * d, (h + 1) * d)
        s = _dot_nt(q_ref[:, cols], k_ref[:, cols]) * scale
        s = s - jnp.max(s, axis=-1, keepdims=True)
        p = jnp.exp(s)
        p = p / jnp.sum(p, axis=-1, keepdims=True)
        o_ref[:, cols] = _dot(p.astype(BF16), v_ref[:, cols]).astype(o_ref.dtype)


def _memattn(proj, mkv, *, batch, seq, heads, d, col_q, tq, nh):
    nq = seq // tq
    w = nh * d
    kern = functools.partial(_memattn_kernel, nh=nh, d=d, scale=1.0 / math.sqrt(d))
    return pl.pallas_call(
        kern,
        grid=(batch, heads // nh, nq),
        in_specs=[
            pl.BlockSpec((tq, w), lambda b, h, i: (b * nq + i, col_q // w + h)),
            pl.BlockSpec((N_MEM, w), lambda b, h, i: (b, h)),
            pl.BlockSpec((N_MEM, w), lambda b, h, i: (b, heads // nh + h)),
        ],
        out_specs=pl.BlockSpec((tq, w), lambda b, h, i: (b * nq + i, h)),
        out_shape=jax.ShapeDtypeStruct((batch * seq, heads * d), BF16),
        compiler_params=_params(("parallel", "parallel", "parallel")),
        name="memattn",
    )(proj, mkv, mkv)


def _merge_kernel(oa_ref, ob_ref, om_ref, wa_ref, wb_ref, wm_ref,
                  ga_ref, gb_ref, gm_ref, o_ref):
    acc = _sigmoid(ga_ref[...].astype(F32)) * _dot(oa_ref[...], wa_ref[...])
    acc += _sigmoid(gb_ref[...].astype(F32)) * _dot(ob_ref[...], wb_ref[...])
    acc += _sigmoid(gm_ref[...].astype(F32)) * _dot(om_ref[...], wm_ref[...])
    o_ref[...] = acc.astype(o_ref.dtype)


def _merge(o_a, o_b, o_m, w_a, w_b, w_m, proj, *, col_gates, tm, tn):
    m, d = o_a.shape
    gate_blk = col_gates // tn
    act = lambda: pl.BlockSpec((tm, d), lambda i, j: (i, 0))
    wgt = lambda: pl.BlockSpec((d, tn), lambda i, j: (0, j))
    gate = lambda n: pl.BlockSpec((tm, tn), lambda i, j: (i, gate_blk + n * (d // tn) + j))
    return pl.pallas_call(
        _merge_kernel,
        grid=(m // tm, d // tn),
        in_specs=[act(), act(), act(), wgt(), wgt(), wgt(), gate(0), gate(1), gate(2)],
        out_specs=pl.BlockSpec((tm, tn), lambda i, j: (i, j)),
        out_shape=jax.ShapeDtypeStruct((m, d), BF16),
        compiler_params=_params(("parallel", "parallel")),
        name="merge",
    )(o_a, o_b, o_m, w_a, w_b, w_m, proj, proj, proj)


def _out_kernel(m_ref, w_ref, g_ref, x_ref, o_ref):
    y = _dot(m_ref[...], w_ref[...])
    ms = jnp.mean(y * y, axis=-1, keepdims=True)
    o_ref[...] = x_ref[...] + y * lax.rsqrt(ms + EPS) * g_ref[...]


def _out(merged, w_out, g, x2d, tm):
    m, d = x2d.shape
    return pl.pallas_call(
        _out_kernel,
        grid=(m // tm,),
        in_specs=[
            pl.BlockSpec((tm, d), lambda i: (i, 0)),
            pl.BlockSpec((d, d), lambda i: (0, 0)),
            pl.BlockSpec((1, d), lambda i: (0, 0)),
            pl.BlockSpec((tm, d), lambda i: (i, 0)),
        ],
        out_specs=pl.BlockSpec((tm, d), lambda i: (i, 0)),
        out_shape=jax.ShapeDtypeStruct((m, d), F32),
        compiler_params=_params(("parallel",)),
        name="outproj",
    )(merged, w_out, g.reshape(1, d), x2d)


def kernel(x, mem, norm_pre_g, norm_post_g, norm_mem_g, w_in, gla_a_w2, gla_a_b,
           gla_head_norm_g, w_mem_kv, w_proj_gla, w_proj_sb, w_proj_mem, w_out):
    batch, seq, d = x.shape
    dk_total, dv_total = d // 2, d
    dk, dv = dk_total // GLA_HEADS, dv_total // GLA_HEADS
    sb_heads = d // SB_HEAD_DIM
    mem_d = d // MEM_HEADS

    c_ga = 2 * dk_total + 2 * dv_total

    x2d = x.reshape(batch * seq, d)
    mem2d = mem.reshape(batch * N_MEM, d)

    mkv = _memkv(mem2d, norm_mem_g, w_mem_kv, tn=512)
    w_in_t = w_in.T
    h, ga = _prenorm(x2d, norm_pre_g, w_in_t, c_ga, tm=512)
    proj_a = _inproj(h, w_in_t, row0=0, width=c_ga, tm=1024, tn=1024, name="inproj_a")
    proj_b = _inproj(h, w_in_t, row0=c_ga + GLA_RANK, width=w_in.shape[1] - c_ga - GLA_RANK,
                     tm=1024, tn=1024, name="inproj_b")

    o_a = _gla(proj_a, ga, gla_a_w2, gla_a_b, gla_head_norm_g, batch=batch, seq=seq,
               heads=GLA_HEADS, dk=dk, dv=dv, col_q=0, col_k=dk_total, col_v=2 * dk_total,
               col_g=2 * dk_total + dv_total, cs=128, nh=4)
    o_b = _sb(proj_b, batch=batch, seq=seq, heads=sb_heads, d=SB_HEAD_DIM,
              col_q=0, col_k=d, col_v=2 * d, col_g=3 * d, tq=256, nh=4)
    o_m = _memattn(proj_b, mkv, batch=batch, seq=seq, heads=MEM_HEADS, d=mem_d,
                   col_q=4 * d, tq=256, nh=4)

    merged = _merge(o_a, o_b, o_m, w_proj_gla.astype(BF16), w_proj_sb.astype(BF16),
                    w_proj_mem.astype(BF16), proj_b, col_gates=5 * d, tm=1024, tn=512)
    out = _out(merged, w_out.astype(BF16), norm_post_g, x2d, tm=512)
    return out.reshape(batch, seq, d)
```

```python
import functools
import math

import jax
import jax.numpy as jnp
from jax import lax
from jax.experimental import pallas as pl
from jax.experimental.pallas import tpu as pltpu

F32 = jnp.float32
BF16 = jnp.bfloat16

EPS = 1e-6
N_MEM = 256
GLA_HEADS = 4
GLA_RANK = 16
GLA_TAU = 16.0
SB_HEAD_DIM = 128
MEM_HEADS = 4
N_BRANCHES = 3

LANES = 128
SUBLANES = 8
VMEM_LIMIT = 56 * 1024 * 1024
SB_LOG2_CUTOFF = 90.0 * math.log2(math.e)
SB_EXP2_CLAMP = 126.0


def _params(sem):
    return pltpu.CompilerParams(dimension_semantics=sem, vmem_limit_bytes=VMEM_LIMIT)


def _split_bf16(x):
    hi = x.astype(BF16)
    lo = (x - hi.astype(F32)).astype(BF16)
    return hi, lo


def _dot(a, b):
    return jnp.dot(a, b, preferred_element_type=F32)


def _dot_nt(a, b):
    return lax.dot_general(a, b, (((1,), (1,)), ((), ())), preferred_element_type=F32)


def _dot_tn(a, b):
    return lax.dot_general(a, b, (((0,), (0,)), ((), ())), preferred_element_type=F32)


def _softplus(z):
    return jnp.maximum(z, 0.0) + jnp.log(1.0 + jnp.exp(-jnp.abs(z)))


def _sigmoid(z):
    return 1.0 / (1.0 + jnp.exp(-z))


def _prenorm_kernel(x_ref, g_ref, wga_ref, h_ref, ga_ref):
    x = x_ref[...]
    ms = jnp.mean(x * x, axis=-1, keepdims=True)
    h = (x * lax.rsqrt(ms + EPS) * g_ref[...]).astype(BF16)
    h_ref[...] = h
    ga_ref[...] = _dot_nt(h, wga_ref[...].astype(BF16))


def _prenorm(x2d, g, w_in_t, col_ga, tm):
    m, d = x2d.shape
    return pl.pallas_call(
        _prenorm_kernel,
        grid=(m // tm,),
        in_specs=[
            pl.BlockSpec((tm, d), lambda i: (i, 0)),
            pl.BlockSpec((1, d), lambda i: (0, 0)),
            pl.BlockSpec((LANES, d), lambda i: (col_ga // LANES, 0)),
        ],
        out_specs=[
            pl.BlockSpec((tm, d), lambda i: (i, 0)),
            pl.BlockSpec((tm, LANES), lambda i: (i, 0)),
        ],
        out_shape=[
            jax.ShapeDtypeStruct((m, d), BF16),
            jax.ShapeDtypeStruct((m, LANES), F32),
        ],
        compiler_params=_params(("parallel",)),
        name="prenorm",
    )(x2d, g.reshape(1, d), w_in_t)


CAST_ROWS = 128


def _inproj_kernel(a_ref, w_ref, o_ref, wb_ref):
    @pl.when(pl.program_id(1) == 0)
    def _():
        for r in range(0, wb_ref.shape[0], CAST_ROWS):
            rows = slice(r, r + CAST_ROWS)
            wb_ref[rows, :] = w_ref[rows, :].astype(BF16)

    o_ref[...] = _dot_nt(a_ref[...], wb_ref[...]).astype(o_ref.dtype)


def _inproj(a, w_t, *, row0, width, tm, tn, name):
    m, k = a.shape
    return pl.pallas_call(
        _inproj_kernel,
        grid=(width // tn, m // tm),
        in_specs=[
            pl.BlockSpec((tm, k), lambda j, i: (i, 0)),
            pl.BlockSpec((pl.Element(tn), pl.Element(k)),
                         lambda j, i: ((row0 // SUBLANES + j * (tn // SUBLANES)) * SUBLANES, 0)),
        ],
        out_specs=pl.BlockSpec((tm, tn), lambda j, i: (i, j)),
        out_shape=jax.ShapeDtypeStruct((m, width), BF16),
        scratch_shapes=[pltpu.VMEM((tn, k), BF16)],
        compiler_params=_params(("parallel", "arbitrary")),
        name=name,
    )(a, w_t)


def _memkv_kernel(m_ref, g_ref, w_ref, o_ref):
    x = m_ref[...]
    ms = jnp.mean(x * x, axis=-1, keepdims=True)
    h = (x * lax.rsqrt(ms + EPS) * g_ref[...]).astype(BF16)
    o_ref[...] = _dot(h, w_ref[...].astype(BF16)).astype(o_ref.dtype)


def _memkv(mem2d, g, w, tn):
    m, d = mem2d.shape
    n = w.shape[1]
    return pl.pallas_call(
        _memkv_kernel,
        grid=(n // tn,),
        in_specs=[
            pl.BlockSpec((m, d), lambda j: (0, 0)),
            pl.BlockSpec((1, d), lambda j: (0, 0)),
            pl.BlockSpec((d, tn), lambda j: (0, j)),
        ],
        out_specs=pl.BlockSpec((m, tn), lambda j: (0, j)),
        out_shape=jax.ShapeDtypeStruct((m, n), BF16),
        compiler_params=_params(("parallel",)),
        name="memkv",
    )(mem2d, g.reshape(1, d), w)


def _gla_kernel(q_ref, k_ref, v_ref, gg_ref, ga_ref, w2h_ref, w2l_ref, ab_ref, hn_ref,
                o_ref, st_ref, *, cs, nh, dk, dv, scale):
    @pl.when(pl.program_id(2) == 0)
    def _():
        st_ref[...] = jnp.zeros_like(st_ref)

    ga_hi, ga_lo = _split_bf16(ga_ref[...])
    w2_hi = w2h_ref[...]
    z = _dot(ga_hi, w2_hi) + _dot(ga_lo, w2_hi) + _dot(ga_hi, w2l_ref[...]) + ab_ref[...]
    log_a = -_softplus(-z) * (math.log2(math.e) / GLA_TAU)

    row = lax.broadcasted_iota(jnp.int32, (cs, cs), 0)
    col = lax.broadcasted_iota(jnp.int32, (cs, cs), 1)
    causal = col <= row
    tri = jnp.where(causal, 1.0, 0.0).astype(BF16)
    la_hi, la_lo = _split_bf16(log_a)
    b_all = _dot(tri, la_hi) + _dot(tri, la_lo)

    for h in range(nh):
        kc = slice(h * dk, (h + 1) * dk)
        vc = slice(h * dv, (h + 1) * dv)
        b = b_all[:, kc]
        b_last = b[cs - 1:cs, :]
        b_mid = b[cs // 2 - 1:cs // 2, :]
        q = q_ref[:, kc].astype(F32) * scale
        k = k_ref[:, kc].astype(F32)
        v = v_ref[:, vc]
        q_inter = (q * jnp.exp2(b)).astype(BF16)
        k_carry = (k * jnp.exp2(b_last - b)).astype(BF16)
        q_intra = (q * jnp.exp2(b - b_mid)).astype(BF16)
        k_intra = (k * jnp.exp2(b_mid - b)).astype(BF16)

        attn = jnp.where(causal, _dot_nt(q_intra, k_intra), 0.0).astype(BF16)
        st = st_ref[h]
        o = _dot_nt(q_inter, st.astype(BF16)) + _dot(attn, v)
        st_ref[h] = st * jnp.exp2(b_last) + _dot_tn(v, k_carry)

        ms = jnp.mean(o * o, axis=-1, keepdims=True)
        o = o * lax.rsqrt(ms + EPS) * hn_ref[...]
        gg = gg_ref[:, vc].astype(F32)
        o_ref[:, vc] = (o * (gg * _sigmoid(gg))).astype(o_ref.dtype)


def _gla(proj, ga, w2, a_b, hn_g, *, batch, seq, heads, dk, dv, col_q, col_k,
         col_v, col_g, cs, nh):
    nc = seq // cs
    wk, wv = nh * dk, nh * dv
    w2_hi, w2_lo = _split_bf16(jnp.pad(w2, ((0, LANES - w2.shape[0]), (0, 0))))
    kern = functools.partial(_gla_kernel, cs=cs, nh=nh, dk=dk, dv=dv, scale=dk ** -0.5)
    row = lambda b, c: b * nc + c
    return pl.pallas_call(
        kern,
        grid=(batch, heads // nh, nc),
        in_specs=[
            pl.BlockSpec((cs, wk), lambda b, h, c: (row(b, c), col_q // wk + h)),
            pl.BlockSpec((cs, wk), lambda b, h, c: (row(b, c), col_k // wk + h)),
            pl.BlockSpec((cs, wv), lambda b, h, c: (row(b, c), col_v // wv + h)),
            pl.BlockSpec((cs, wv), lambda b, h, c: (row(b, c), col_g // wv + h)),
            pl.BlockSpec((cs, LANES), lambda b, h, c: (row(b, c), 0)),
            pl.BlockSpec((LANES, wk), lambda b, h, c: (0, h)),
            pl.BlockSpec((LANES, wk), lambda b, h, c: (0, h)),
            pl.BlockSpec((1, wk), lambda b, h, c: (0, h)),
            pl.BlockSpec((1, dv), lambda b, h, c: (0, 0)),
        ],
        out_specs=pl.BlockSpec((cs, wv), lambda b, h, c: (row(b, c), h)),
        out_shape=jax.ShapeDtypeStruct((batch * seq, heads * dv), BF16),
        scratch_shapes=[pltpu.VMEM((nh, dv, dk), F32)],
        compiler_params=_params(("parallel", "parallel", "arbitrary")),
        name="gla",
    )(proj, proj, proj, proj, ga, w2_hi, w2_lo, a_b.reshape(1, -1), hn_g.reshape(1, -1))


def _sb_kernel(q_ref, k_ref, v_ref, sg_ref, o_ref, carry_ref, acc_ref, *, tq, nh, d, scale2):
    qi = pl.program_id(2)
    groups = tq // LANES

    r = lax.broadcasted_iota(jnp.int32, (2 * LANES, 2 * LANES), 0) % LANES
    c = lax.broadcasted_iota(jnp.int32, (2 * LANES, 2 * LANES), 1)
    cum_w = jnp.where((c >= LANES) | (r > c), 1.0, 0.0).astype(BF16)

    carry_ref[...] = jnp.zeros_like(carry_ref)
    acc_ref[...] = jnp.zeros_like(acc_ref)

    def sweep(key_start, masked):
        if masked:
            t_idx = lax.broadcasted_iota(jnp.int32, (tq, tq), 0)
            s_idx = lax.broadcasted_iota(jnp.int32, (tq, tq), 1)
            strict = s_idx < t_idx
        for h in range(nh):
            cols = slice(h * d, (h + 1) * d)
            kblk = k_ref[pl.ds(key_start, tq), cols]
            vblk = v_ref[pl.ds(key_start, tq), cols]
            u = _dot_nt(q_ref[:, cols], kblk) * scale2
            sp = jnp.maximum(jnp.log2(1.0 + jnp.exp2(jnp.minimum(u, SB_EXP2_CLAMP))), u)
            sp_m = jnp.where(strict, sp, 0.0) if masked else sp
            carry = carry_ref[h]
            after = [None] * groups
            for g in reversed(range(groups)):
                hi, lo = _split_bf16(sp_m[:, g * LANES:(g + 1) * LANES])
                ct = _dot(jnp.concatenate([hi, lo], axis=1), cum_w)
                after[g] = ct[:, :LANES] + carry
                carry = carry + ct[:, LANES:]
            carry_ref[h] = carry
            w = jnp.exp2(u - sp - jnp.concatenate(after, axis=1))
            if masked:
                w = jnp.where(strict, w, 0.0)
            acc_ref[h] += _dot(w.astype(BF16), vblk)

    def least_carry():
        m = carry_ref[0]
        for h in range(1, nh):
            m = jnp.minimum(m, carry_ref[h])
        return jnp.min(m)

    sweep(pl.multiple_of(qi * tq, tq), True)

    def cond(state):
        blk, low = state
        return jnp.logical_and(blk >= 0, low < SB_LOG2_CUTOFF)

    def body(state):
        blk, _ = state
        sweep(pl.multiple_of(blk * tq, tq), False)
        return blk - 1, least_carry()

    lax.while_loop(cond, body, (qi - 1, least_carry()))

    for h in range(nh):
        cols = slice(h * d, (h + 1) * d)
        sg = sg_ref[:, cols].astype(F32)
        o_ref[:, cols] = (acc_ref[h] * (sg * _sigmoid(sg))).astype(o_ref.dtype)


def _sb(proj, *, batch, seq, heads, d, col_q, col_k, col_v, col_g, tq, nh):
    nq = seq // tq
    w = nh * d
    kern = functools.partial(_sb_kernel, tq=tq, nh=nh, d=d, scale2=math.log2(math.e) / math.sqrt(d))
    return pl.pallas_call(
        kern,
        grid=(batch, heads // nh, nq),
        in_specs=[
            pl.BlockSpec((tq, w), lambda b, h, i: (b * nq + i, col_q // w + h)),
            pl.BlockSpec((seq, w), lambda b, h, i: (b, col_k // w + h)),
            pl.BlockSpec((seq, w), lambda b, h, i: (b, col_v // w + h)),
            pl.BlockSpec((tq, w), lambda b, h, i: (b * nq + i, col_g // w + h)),
        ],
        out_specs=pl.BlockSpec((tq, w), lambda b, h, i: (b * nq + i, h)),
        out_shape=jax.ShapeDtypeStruct((batch * seq, heads * d), BF16),
        scratch_shapes=[pltpu.VMEM((nh, tq, LANES), F32), pltpu.VMEM((nh, tq, d), F32)],
        compiler_params=_params(("parallel", "parallel", "arbitrary")),
        name="stickbreak",
    )(proj, proj, proj, proj)


def _memattn_kernel(q_ref, k_ref, v_ref, o_ref, *, nh, d, scale):
    for h in range(nh):
        cols = slice(h * d, (h + 1) * d)
        s = _dot_nt(q_ref[:, cols], k_ref[:, cols]) * scale
        s = s - jnp.max(s, axis=-1, keepdims=True)
        p = jnp.exp(s)
        p = p / jnp.sum(p, axis=-1, keepdims=True)
        o_ref[:, cols] = _dot(p.astype(BF16), v_ref[:, cols]).astype(o_ref.dtype)


def _memattn(proj, mkv, *, batch, seq, heads, d, col_q, tq, nh):
    nq = seq // tq
    w = nh * d
    kern = functools.partial(_memattn_kernel, nh=nh, d=d, scale=1.0 / math.sqrt(d))
    return pl.pallas_call(
        kern,
        grid=(batch, heads // nh, nq),
        in_specs=[
            pl.BlockSpec((tq, w), lambda b, h, i: (b * nq + i, col_q // w + h)),
            pl.BlockSpec((N_MEM, w), lambda b, h, i: (b, h)),
            pl.BlockSpec((N_MEM, w), lambda b, h, i: (b, heads // nh + h)),
        ],
        out_specs=pl.BlockSpec((tq, w), lambda b, h, i: (b * nq + i, h)),
        out_shape=jax.ShapeDtypeStruct((batch * seq, heads * d), BF16),
        compiler_params=_params(("parallel", "parallel", "parallel")),
        name="memattn",
    )(proj, mkv, mkv)


def _merge_kernel(oa_ref, ob_ref, om_ref, wa_ref, wb_ref, wm_ref,
                  ga_ref, gb_ref, gm_ref, o_ref):
    acc = _sigmoid(ga_ref[...].astype(F32)) * _dot(oa_ref[...], wa_ref[...])
    acc += _sigmoid(gb_ref[...].astype(F32)) * _dot(ob_ref[...], wb_ref[...])
    acc += _sigmoid(gm_ref[...].astype(F32)) * _dot(om_ref[...], wm_ref[...])
    o_ref[...] = acc.astype(o_ref.dtype)


def _merge(o_a, o_b, o_m, w_a, w_b, w_m, proj, *, col_gates, tm, tn):
    m, d = o_a.shape
    gate_blk = col_gates // tn
    act = lambda: pl.BlockSpec((tm, d), lambda i, j: (i, 0))
    wgt = lambda: pl.BlockSpec((d, tn), lambda i, j: (0, j))
    gate = lambda n: pl.BlockSpec((tm, tn), lambda i, j: (i, gate_blk + n * (d // tn) + j))
    return pl.pallas_call(
        _merge_kernel,
        grid=(m // tm, d // tn),
        in_specs=[act(), act(), act(), wgt(), wgt(), wgt(), gate(0), gate(1), gate(2)],
        out_specs=pl.BlockSpec((tm, tn), lambda i, j: (i, j)),
        out_shape=jax.ShapeDtypeStruct((m, d), BF16),
        compiler_params=_params(("parallel", "parallel")),
        name="merge",
    )(o_a, o_b, o_m, w_a, w_b, w_m, proj, proj, proj)


def _out_kernel(m_ref, w_ref, g_ref, x_ref, o_ref):
    y = _dot(m_ref[...], w_ref[...])
    ms = jnp.mean(y * y, axis=-1, keepdims=True)
    o_ref[...] = x_ref[...] + y * lax.rsqrt(ms + EPS) * g_ref[...]


def _out(merged, w_out, g, x2d, tm):
    m, d = x2d.shape
    return pl.pallas_call(
        _out_kernel,
        grid=(m // tm,),
        in_specs=[
            pl.BlockSpec((tm, d), lambda i: (i, 0)),
            pl.BlockSpec((d, d), lambda i: (0, 0)),
            pl.BlockSpec((1, d), lambda i: (0, 0)),
            pl.BlockSpec((tm, d), lambda i: (i, 0)),
        ],
        out_specs=pl.BlockSpec((tm, d), lambda i: (i, 0)),
        out_shape=jax.ShapeDtypeStruct((m, d), F32),
        compiler_params=_params(("parallel",)),
        name="outproj",
    )(merged, w_out, g.reshape(1, d), x2d)


def kernel(x, mem, norm_pre_g, norm_post_g, norm_mem_g, w_in, gla_a_w2, gla_a_b,
           gla_head_norm_g, w_mem_kv, w_proj_gla, w_proj_sb, w_proj_mem, w_out):
    batch, seq, d = x.shape
    dk_total, dv_total = d // 2, d
    dk, dv = dk_total // GLA_HEADS, dv_total // GLA_HEADS
    sb_heads = d // SB_HEAD_DIM
    mem_d = d // MEM_HEADS

    c_ga = 2 * dk_total + 2 * dv_total

    x2d = x.reshape(batch * seq, d)
    mem2d = mem.reshape(batch * N_MEM, d)

    mkv = _memkv(mem2d, norm_mem_g, w_mem_kv, tn=512)
    w_in_t = w_in.T
    h, ga = _prenorm(x2d, norm_pre_g, w_in_t, c_ga, tm=512)
    proj_a = _inproj(h, w_in_t, row0=0, width=c_ga, tm=1024, tn=1024, name="inproj_a")
    proj_b = _inproj(h, w_in_t, row0=c_ga + GLA_RANK, width=w_in.shape[1] - c_ga - GLA_RANK,
                     tm=2048, tn=1024, name="inproj_b")

    o_a = _gla(proj_a, ga, gla_a_w2, gla_a_b, gla_head_norm_g, batch=batch, seq=seq,
               heads=GLA_HEADS, dk=dk, dv=dv, col_q=0, col_k=dk_total, col_v=2 * dk_total,
               col_g=2 * dk_total + dv_total, cs=128, nh=4)
    o_b = _sb(proj_b, batch=batch, seq=seq, heads=sb_heads, d=SB_HEAD_DIM,
              col_q=0, col_k=d, col_v=2 * d, col_g=3 * d, tq=256, nh=8)
    o_m = _memattn(proj_b, mkv, batch=batch, seq=seq, heads=MEM_HEADS, d=mem_d,
                   col_q=4 * d, tq=256, nh=4)

    merged = _merge(o_a, o_b, o_m, w_proj_gla.astype(BF16), w_proj_sb.astype(BF16),
                    w_proj_mem.astype(BF16), proj_b, col_gates=5 * d, tm=1024, tn=512)
    out = _out(merged, w_out.astype(BF16), norm_post_g, x2d, tm=512)
    return out.reshape(batch, seq, d)
```

```python
import functools
import math

import jax
import jax.numpy as jnp
from jax import lax
from jax.experimental import pallas as pl
from jax.experimental.pallas import tpu as pltpu

F32 = jnp.float32
BF16 = jnp.bfloat16

EPS = 1e-6
N_MEM = 256
GLA_HEADS = 4
GLA_RANK = 16
GLA_TAU = 16.0
SB_HEAD_DIM = 128
MEM_HEADS = 4
N_BRANCHES = 3

LANES = 128
SUBLANES = 8
VMEM_LIMIT = 56 * 1024 * 1024
SB_LOG2_CUTOFF = 90.0 * math.log2(math.e)
SB_EXP2_CLAMP = 126.0


def _params(sem):
    return pltpu.CompilerParams(dimension_semantics=sem, vmem_limit_bytes=VMEM_LIMIT)


def _split_bf16(x):
    hi = x.astype(BF16)
    lo = (x - hi.astype(F32)).astype(BF16)
    return hi, lo


def _dot(a, b):
    return jnp.dot(a, b, preferred_element_type=F32)


def _dot_nt(a, b):
    return lax.dot_general(a, b, (((1,), (1,)), ((), ())), preferred_element_type=F32)


def _dot_tn(a, b):
    return lax.dot_general(a, b, (((0,), (0,)), ((), ())), preferred_element_type=F32)


def _softplus(z):
    return jnp.maximum(z, 0.0) + jnp.log(1.0 + jnp.exp(-jnp.abs(z)))


def _sigmoid(z):
    return 1.0 / (1.0 + jnp.exp(-z))


def _prenorm_kernel(x_ref, g_ref, wga_ref, h_ref, ga_ref):
    x = x_ref[...]
    ms = jnp.mean(x * x, axis=-1, keepdims=True)
    h = (x * lax.rsqrt(ms + EPS) * g_ref[...]).astype(BF16)
    h_ref[...] = h
    ga_ref[...] = _dot_nt(h, wga_ref[...].astype(BF16))


def _prenorm(x2d, g, w_in_t, col_ga, tm):
    m, d = x2d.shape
    return pl.pallas_call(
        _prenorm_kernel,
        grid=(m // tm,),
        in_specs=[
            pl.BlockSpec((tm, d), lambda i: (i, 0)),
            pl.BlockSpec((1, d), lambda i: (0, 0)),
            pl.BlockSpec((LANES, d), lambda i: (col_ga // LANES, 0)),
        ],
        out_specs=[
            pl.BlockSpec((tm, d), lambda i: (i, 0)),
            pl.BlockSpec((tm, LANES), lambda i: (i, 0)),
        ],
        out_shape=[
            jax.ShapeDtypeStruct((m, d), BF16),
            jax.ShapeDtypeStruct((m, LANES), F32),
        ],
        compiler_params=_params(("parallel",)),
        name="prenorm",
    )(x2d, g.reshape(1, d), w_in_t)


CAST_ROWS = 128


def _inproj_kernel(a_ref, w_ref, o_ref, wb_ref):
    @pl.when(pl.program_id(1) == 0)
    def _():
        for r in range(0, wb_ref.shape[0], CAST_ROWS):
            rows = slice(r, r + CAST_ROWS)
            wb_ref[rows, :] = w_ref[rows, :].astype(BF16)

    o_ref[...] = _dot_nt(a_ref[...], wb_ref[...]).astype(o_ref.dtype)


def _inproj(a, w_t, *, row0, width, tm, tn, name):
    m, k = a.shape
    return pl.pallas_call(
        _inproj_kernel,
        grid=(width // tn, m // tm),
        in_specs=[
            pl.BlockSpec((tm, k), lambda j, i: (i, 0)),
            pl.BlockSpec((pl.Element(tn), pl.Element(k)),
                         lambda j, i: ((row0 // SUBLANES + j * (tn // SUBLANES)) * SUBLANES, 0)),
        ],
        out_specs=pl.BlockSpec((tm, tn), lambda j, i: (i, j)),
        out_shape=jax.ShapeDtypeStruct((m, width), BF16),
        scratch_shapes=[pltpu.VMEM((tn, k), BF16)],
        compiler_params=_params(("parallel", "arbitrary")),
        name=name,
    )(a, w_t)


def _memkv_kernel(m_ref, g_ref, w_ref, o_ref):
    x = m_ref[...]
    ms = jnp.mean(x * x, axis=-1, keepdims=True)
    h = (x * lax.rsqrt(ms + EPS) * g_ref[...]).astype(BF16)
    o_ref[...] = _dot(h, w_ref[...].astype(BF16)).astype(o_ref.dtype)


def _memkv(mem2d, g, w, tn):
    m, d = mem2d.shape
    n = w.shape[1]
    return pl.pallas_call(
        _memkv_kernel,
        grid=(n // tn,),
        in_specs=[
            pl.BlockSpec((m, d), lambda j: (0, 0)),
            pl.BlockSpec((1, d), lambda j: (0, 0)),
            pl.BlockSpec((d, tn), lambda j: (0, j)),
        ],
        out_specs=pl.BlockSpec((m, tn), lambda j: (0, j)),
        out_shape=jax.ShapeDtypeStruct((m, n), BF16),
        compiler_params=_params(("parallel",)),
        name="memkv",
    )(mem2d, g.reshape(1, d), w)


def _gla_kernel(q_ref, k_ref, v_ref, gg_ref, ga_ref, w2h_ref, w2l_ref, ab_ref, hn_ref,
                o_ref, st_ref, *, cs, nh, dk, dv, scale):
    @pl.when(pl.program_id(2) == 0)
    def _():
        st_ref[...] = jnp.zeros_like(st_ref)

    ga_hi, ga_lo = _split_bf16(ga_ref[...])
    w2_hi = w2h_ref[...]
    z = _dot(ga_hi, w2_hi) + _dot(ga_lo, w2_hi) + _dot(ga_hi, w2l_ref[...]) + ab_ref[...]
    log_a = -_softplus(-z) * (math.log2(math.e) / GLA_TAU)

    row = lax.broadcasted_iota(jnp.int32, (cs, cs), 0)
    col = lax.broadcasted_iota(jnp.int32, (cs, cs), 1)
    causal = col <= row
    tri = jnp.where(causal, 1.0, 0.0).astype(BF16)
    la_hi, la_lo = _split_bf16(log_a)
    b_all = _dot(tri, la_hi) + _dot(tri, la_lo)

    for h in range(nh):
        kc = slice(h * dk, (h + 1) * dk)
        vc = slice(h * dv, (h + 1) * dv)
        b = b_all[:, kc]
        b_last = b[cs - 1:cs, :]
        b_mid = b[cs // 2 - 1:cs // 2, :]
        q = q_ref[:, kc].astype(F32) * scale
        k = k_ref[:, kc].astype(F32)
        v = v_ref[:, vc]
        q_inter = (q * jnp.exp2(b)).astype(BF16)
        k_carry = (k * jnp.exp2(b_last - b)).astype(BF16)
        q_intra = (q * jnp.exp2(b - b_mid)).astype(BF16)
        k_intra = (k * jnp.exp2(b_mid - b)).astype(BF16)

        attn = jnp.where(causal, _dot_nt(q_intra, k_intra), 0.0).astype(BF16)
        st = st_ref[h]
        o = _dot_nt(q_inter, st.astype(BF16)) + _dot(attn, v)
        st_ref[h] = st * jnp.exp2(b_last) + _dot_tn(v, k_carry)

        ms = jnp.mean(o * o, axis=-1, keepdims=True)
        o = o * lax.rsqrt(ms + EPS) * hn_ref[...]
        gg = gg_ref[:, vc].astype(F32)
        o_ref[:, vc] = (o * (gg * _sigmoid(gg))).astype(o_ref.dtype)


def _gla(proj, ga, w2, a_b, hn_g, *, batch, seq, heads, dk, dv, col_q, col_k,
         col_v, col_g, cs, nh):
    nc = seq // cs
    wk, wv = nh * dk, nh * dv
    w2_hi, w2_lo = _split_bf16(jnp.pad(w2, ((0, LANES - w2.shape[0]), (0, 0))))
    kern = functools.partial(_gla_kernel, cs=cs, nh=nh, dk=dk, dv=dv, scale=dk ** -0.5)
    row = lambda b, c: b * nc + c
    return pl.pallas_call(
        kern,
        grid=(batch, heads // nh, nc),
        in_specs=[
            pl.BlockSpec((cs, wk), lambda b, h, c: (row(b, c), col_q // wk + h)),
            pl.BlockSpec((cs, wk), lambda b, h, c: (row(b, c), col_k // wk + h)),
            pl.BlockSpec((cs, wv), lambda b, h, c: (row(b, c), col_v // wv + h)),
            pl.BlockSpec((cs, wv), lambda b, h, c: (row(b, c), col_g // wv + h)),
            pl.BlockSpec((cs, LANES), lambda b, h, c: (row(b, c), 0)),
            pl.BlockSpec((LANES, wk), lambda b, h, c: (0, h)),
            pl.BlockSpec((LANES, wk), lambda b, h, c: (0, h)),
            pl.BlockSpec((1, wk), lambda b, h, c: (0, h)),
            pl.BlockSpec((1, dv), lambda b, h, c: (0, 0)),
        ],
        out_specs=pl.BlockSpec((cs, wv), lambda b, h, c: (row(b, c), h)),
        out_shape=jax.ShapeDtypeStruct((batch * seq, heads * dv), BF16),
        scratch_shapes=[pltpu.VMEM((nh, dv, dk), F32)],
        compiler_params=_params(("parallel", "parallel", "arbitrary")),
        name="gla",
    )(proj, proj, proj, proj, ga, w2_hi, w2_lo, a_b.reshape(1, -1), hn_g.reshape(1, -1))


def _sb_kernel(q_ref, k_ref, v_ref, sg_ref, o_ref, carry_ref, acc_ref, *, tq, nh, d, scale2):
    qi = pl.program_id(2)
    q0 = pl.multiple_of(qi * tq, tq)

    r = lax.broadcasted_iota(jnp.int32, (2 * LANES, 2 * LANES), 0) % LANES
    c = lax.broadcasted_iota(jnp.int32, (2 * LANES, 2 * LANES), 1)
    cum_w = jnp.where((c >= LANES) | (r > c), 1.0, 0.0).astype(BF16)

    carry_ref[...] = jnp.zeros_like(carry_ref)
    acc_ref[...] = jnp.zeros_like(acc_ref)

    def sweep(key_start, nk, r0=0, diag_off=None):
        rows = slice(r0, tq)
        nr = tq - r0
        if diag_off is not None:
            t_idx = r0 + lax.broadcasted_iota(jnp.int32, (nr, nk), 0)
            s_idx = diag_off + lax.broadcasted_iota(jnp.int32, (nr, nk), 1)
            strict = s_idx < t_idx
        for h in range(nh):
            cols = slice(h * d, (h + 1) * d)
            kblk = k_ref[pl.ds(key_start, nk), cols]
            vblk = v_ref[pl.ds(key_start, nk), cols]
            u = _dot_nt(q_ref[rows, cols], kblk) * scale2
            sp = jnp.maximum(jnp.log2(1.0 + jnp.exp2(jnp.minimum(u, SB_EXP2_CLAMP))), u)
            sp_m = sp if diag_off is None else jnp.where(strict, sp, 0.0)
            carry = carry_ref[h, rows]
            groups = nk // LANES
            after = [None] * groups
            for g in reversed(range(groups)):
                hi, lo = _split_bf16(sp_m[:, g * LANES:(g + 1) * LANES])
                ct = _dot(jnp.concatenate([hi, lo], axis=1), cum_w)
                after[g] = ct[:, :LANES] + carry
                carry = carry + ct[:, LANES:]
            carry_ref[h, rows] = carry
            after = after[0] if groups == 1 else jnp.concatenate(after, axis=1)
            w = jnp.exp2(u - sp - after)
            if diag_off is not None:
                w = jnp.where(strict, w, 0.0)
            acc_ref[h, rows] += _dot(w.astype(BF16), vblk)

    def least_carry():
        m = carry_ref[0]
        for h in range(1, nh):
            m = jnp.minimum(m, carry_ref[h])
        return jnp.min(m)

    sweep(q0, tq, diag_off=0)

    def cond(state):
        blk, low = state
        return jnp.logical_and(blk >= 0, low < SB_LOG2_CUTOFF)

    def body(state):
        blk, _ = state
        sweep(pl.multiple_of(blk * tq, tq), tq)
        return blk - 1, least_carry()

    lax.while_loop(cond, body, (qi - 1, least_carry()))

    for h in range(nh):
        cols = slice(h * d, (h + 1) * d)
        sg = sg_ref[:, cols].astype(F32)
        o_ref[:, cols] = (acc_ref[h] * (sg * _sigmoid(sg))).astype(o_ref.dtype)


def _sb(proj, *, batch, seq, heads, d, col_q, col_k, col_v, col_g, tq, nh):
    nq = seq // tq
    w = nh * d
    kern = functools.partial(_sb_kernel, tq=tq, nh=nh, d=d, scale2=math.log2(math.e) / math.sqrt(d))
    return pl.pallas_call(
        kern,
        grid=(batch, heads // nh, nq),
        in_specs=[
            pl.BlockSpec((tq, w), lambda b, h, i: (b * nq + i, col_q // w + h)),
            pl.BlockSpec((seq, w), lambda b, h, i: (b, col_k // w + h)),
            pl.BlockSpec((seq, w), lambda b, h, i: (b, col_v // w + h)),
            pl.BlockSpec((tq, w), lambda b, h, i: (b * nq + i, col_g // w + h)),
        ],
        out_specs=pl.BlockSpec((tq, w), lambda b, h, i: (b * nq + i, h)),
        out_shape=jax.ShapeDtypeStruct((batch * seq, heads * d), BF16),
        scratch_shapes=[pltpu.VMEM((nh, tq, LANES), F32), pltpu.VMEM((nh, tq, d), F32)],
        compiler_params=_params(("parallel", "parallel", "arbitrary")),
        name="stickbreak",
    )(proj, proj, proj, proj)


def _memattn_kernel(q_ref, k_ref, v_ref, o_ref, *, nh, d, scale):
    for h in range(nh):
        cols = slice(h * d, (h + 1) * d)
        s = _dot_nt(q_ref[:, cols], k_ref[:, cols]) * scale
        s = s - jnp.max(s, axis=-1, keepdims=True)
        p = jnp.exp(s)
        p = p / jnp.sum(p, axis=-1, keepdims=True)
        o_ref[:, cols] = _dot(p.astype(BF16), v_ref[:, cols]).astype(o_ref.dtype)


def _memattn(proj, mkv, *, batch, seq, heads, d, col_q, tq, nh):
    nq = seq // tq
    w = nh * d
    kern = functools.partial(_memattn_kernel, nh=nh, d=d, scale=1.0 / math.sqrt(d))
    return pl.pallas_call(
        kern,
        grid=(batch, heads // nh, nq),
        in_specs=[
            pl.BlockSpec((tq, w), lambda b, h, i: (b * nq + i, col_q // w + h)),
            pl.BlockSpec((N_MEM, w), lambda b, h, i: (b, h)),
            pl.BlockSpec((N_MEM, w), lambda b, h, i: (b, heads // nh + h)),
        ],
        out_specs=pl.BlockSpec((tq, w), lambda b, h, i: (b * nq + i, h)),
        out_shape=jax.ShapeDtypeStruct((batch * seq, heads * d), BF16),
        compiler_params=_params(("parallel", "parallel", "parallel")),
        name="memattn",
    )(proj, mkv, mkv)


def _merge_kernel(oa_ref, ob_ref, om_ref, wa_ref, wb_ref, wm_ref,
                  ga_ref, gb_ref, gm_ref, o_ref):
    acc = _sigmoid(ga_ref[...].astype(F32)) * _dot(oa_ref[...], wa_ref[...])
    acc += _sigmoid(gb_ref[...].astype(F32)) * _dot(ob_ref[...], wb_ref[...])
    acc += _sigmoid(gm_ref[...].astype(F32)) * _dot(om_ref[...], wm_ref[...])
    o_ref[...] = acc.astype(o_ref.dtype)


def _merge(o_a, o_b, o_m, w_a, w_b, w_m, proj, *, col_gates, tm, tn):
    m, d = o_a.shape
    gate_blk = col_gates // tn
    act = lambda: pl.BlockSpec((tm, d), lambda i, j: (i, 0))
    wgt = lambda: pl.BlockSpec((d, tn), lambda i, j: (0, j))
    gate = lambda n: pl.BlockSpec((tm, tn), lambda i, j: (i, gate_blk + n * (d // tn) + j))
    return pl.pallas_call(
        _merge_kernel,
        grid=(m // tm, d // tn),
        in_specs=[act(), act(), act(), wgt(), wgt(), wgt(), gate(0), gate(1), gate(2)],
        out_specs=pl.BlockSpec((tm, tn), lambda i, j: (i, j)),
        out_shape=jax.ShapeDtypeStruct((m, d), BF16),
        compiler_params=_params(("parallel", "parallel")),
        name="merge",
    )(o_a, o_b, o_m, w_a, w_b, w_m, proj, proj, proj)


def _out_kernel(m_ref, w_ref, g_ref, x_ref, o_ref):
    y = _dot(m_ref[...], w_ref[...])
    ms = jnp.mean(y * y, axis=-1, keepdims=True)
    o_ref[...] = x_ref[...] + y * lax.rsqrt(ms + EPS) * g_ref[...]


def _out(merged, w_out, g, x2d, tm):
    m, d = x2d.shape
    return pl.pallas_call(
        _out_kernel,
        grid=(m // tm,),
        in_specs=[
            pl.BlockSpec((tm, d), lambda i: (i, 0)),
            pl.BlockSpec((d, d), lambda i: (0, 0)),
            pl.BlockSpec((1, d), lambda i: (0, 0)),
            pl.BlockSpec((tm, d), lambda i: (i, 0)),
        ],
        out_specs=pl.BlockSpec((tm, d), lambda i: (i, 0)),
        out_shape=jax.ShapeDtypeStruct((m, d), F32),
        compiler_params=_params(("parallel",)),
        name="outproj",
    )(merged, w_out, g.reshape(1, d), x2d)


def kernel(x, mem, norm_pre_g, norm_post_g, norm_mem_g, w_in, gla_a_w2, gla_a_b,
           gla_head_norm_g, w_mem_kv, w_proj_gla, w_proj_sb, w_proj_mem, w_out):
    batch, seq, d = x.shape
    dk_total, dv_total = d // 2, d
    dk, dv = dk_total // GLA_HEADS, dv_total // GLA_HEADS
    sb_heads = d // SB_HEAD_DIM
    mem_d = d // MEM_HEADS

    c_ga = 2 * dk_total + 2 * dv_total

    x2d = x.reshape(batch * seq, d)
    mem2d = mem.reshape(batch * N_MEM, d)

    mkv = _memkv(mem2d, norm_mem_g, w_mem_kv, tn=512)
    w_in_t = w_in.T
    h, ga = _prenorm(x2d, norm_pre_g, w_in_t, c_ga, tm=512)
    proj_a = _inproj(h, w_in_t, row0=0, width=c_ga, tm=2048, tn=1024, name="inproj_a")
    proj_b = _inproj(h, w_in_t, row0=c_ga + GLA_RANK, width=w_in.shape[1] - c_ga - GLA_RANK,
                     tm=2048, tn=1024, name="inproj_b")

    o_a = _gla(proj_a, ga, gla_a_w2, gla_a_b, gla_head_norm_g, batch=batch, seq=seq,
               heads=GLA_HEADS, dk=dk, dv=dv, col_q=0, col_k=dk_total, col_v=2 * dk_total,
               col_g=2 * dk_total + dv_total, cs=256, nh=4)
    o_b = _sb(proj_b, batch=batch, seq=seq, heads=sb_heads, d=SB_HEAD_DIM,
              col_q=0, col_k=d, col_v=2 * d, col_g=3 * d, tq=256, nh=8)
    o_m = _memattn(proj_b, mkv, batch=batch, seq=seq, heads=MEM_HEADS, d=mem_d,
                   col_q=4 * d, tq=256, nh=4)

    merged = _merge(o_a, o_b, o_m, w_proj_gla.astype(BF16), w_proj_sb.astype(BF16),
                    w_proj_mem.astype(BF16), proj_b, col_gates=5 * d, tm=1024, tn=512)
    out = _out(merged, w_out.astype(BF16), norm_post_g, x2d, tm=512)
    return out.reshape(batch, seq, d)
```

```python
import functools
import math

import jax
import jax.numpy as jnp
from jax import lax
from jax.experimental import pallas as pl
from jax.experimental.pallas import tpu as pltpu

F32 = jnp.float32
BF16 = jnp.bfloat16

EPS = 1e-6
N_MEM = 256
GLA_HEADS = 4
GLA_RANK = 16
GLA_TAU = 16.0
SB_HEAD_DIM = 128
MEM_HEADS = 4
N_BRANCHES = 3

LANES = 128
SUBLANES = 8
VMEM_LIMIT = 56 * 1024 * 1024
SB_LOG2_CUTOFF = 90.0 * math.log2(math.e)
SB_EXP2_CLAMP = 126.0


def _params(sem):
    return pltpu.CompilerParams(dimension_semantics=sem, vmem_limit_bytes=VMEM_LIMIT)


def _split_bf16(x):
    hi = x.astype(BF16)
    lo = (x - hi.astype(F32)).astype(BF16)
    return hi, lo


def _dot(a, b):
    return jnp.dot(a, b, preferred_element_type=F32)


def _dot_nt(a, b):
    return lax.dot_general(a, b, (((1,), (1,)), ((), ())), preferred_element_type=F32)


def _dot_tn(a, b):
    return lax.dot_general(a, b, (((0,), (0,)), ((), ())), preferred_element_type=F32)


def _softplus(z):
    return jnp.maximum(z, 0.0) + jnp.log(1.0 + jnp.exp(-jnp.abs(z)))


def _sigmoid(z):
    return 1.0 / (1.0 + jnp.exp(-z))


def _prenorm_kernel(x_ref, g_ref, wga_ref, h_ref, ga_ref):
    x = x_ref[...]
    ms = jnp.mean(x * x, axis=-1, keepdims=True)
    h = (x * lax.rsqrt(ms + EPS) * g_ref[...]).astype(BF16)
    h_ref[...] = h
    ga_ref[...] = _dot_nt(h, wga_ref[...].astype(BF16))


def _prenorm(x2d, g, w_in_t, col_ga, tm):
    m, d = x2d.shape
    return pl.pallas_call(
        _prenorm_kernel,
        grid=(m // tm,),
        in_specs=[
            pl.BlockSpec((tm, d), lambda i: (i, 0)),
            pl.BlockSpec((1, d), lambda i: (0, 0)),
            pl.BlockSpec((LANES, d), lambda i: (col_ga // LANES, 0)),
        ],
        out_specs=[
            pl.BlockSpec((tm, d), lambda i: (i, 0)),
            pl.BlockSpec((tm, LANES), lambda i: (i, 0)),
        ],
        out_shape=[
            jax.ShapeDtypeStruct((m, d), BF16),
            jax.ShapeDtypeStruct((m, LANES), F32),
        ],
        compiler_params=_params(("parallel",)),
        name="prenorm",
    )(x2d, g.reshape(1, d), w_in_t)


CAST_ROWS = 128


def _inproj_kernel(a_ref, w_ref, o_ref, wb_ref):
    @pl.when(pl.program_id(1) == 0)
    def _():
        for r in range(0, wb_ref.shape[0], CAST_ROWS):
            rows = slice(r, r + CAST_ROWS)
            wb_ref[rows, :] = w_ref[rows, :].astype(BF16)

    o_ref[...] = _dot_nt(a_ref[...], wb_ref[...]).astype(o_ref.dtype)


def _inproj(a, w_t, *, row0, width, tm, tn, name):
    m, k = a.shape
    return pl.pallas_call(
        _inproj_kernel,
        grid=(width // tn, m // tm),
        in_specs=[
            pl.BlockSpec((tm, k), lambda j, i: (i, 0)),
            pl.BlockSpec((pl.Element(tn), pl.Element(k)),
                         lambda j, i: ((row0 // SUBLANES + j * (tn // SUBLANES)) * SUBLANES, 0)),
        ],
        out_specs=pl.BlockSpec((tm, tn), lambda j, i: (i, j)),
        out_shape=jax.ShapeDtypeStruct((m, width), BF16),
        scratch_shapes=[pltpu.VMEM((tn, k), BF16)],
        compiler_params=_params(("parallel", "arbitrary")),
        name=name,
    )(a, w_t)


def _memkv_kernel(m_ref, g_ref, w_ref, o_ref):
    x = m_ref[...]
    ms = jnp.mean(x * x, axis=-1, keepdims=True)
    h = (x * lax.rsqrt(ms + EPS) * g_ref[...]).astype(BF16)
    o_ref[...] = _dot(h, w_ref[...].astype(BF16)).astype(o_ref.dtype)


def _memkv(mem2d, g, w, tn):
    m, d = mem2d.shape
    n = w.shape[1]
    return pl.pallas_call(
        _memkv_kernel,
        grid=(n // tn,),
        in_specs=[
            pl.BlockSpec((m, d), lambda j: (0, 0)),
            pl.BlockSpec((1, d), lambda j: (0, 0)),
            pl.BlockSpec((d, tn), lambda j: (0, j)),
        ],
        out_specs=pl.BlockSpec((m, tn), lambda j: (0, j)),
        out_shape=jax.ShapeDtypeStruct((m, n), BF16),
        compiler_params=_params(("parallel",)),
        name="memkv",
    )(mem2d, g.reshape(1, d), w)


def _gla_kernel(q_ref, k_ref, v_ref, gg_ref, ga_ref, w2h_ref, w2l_ref, ab_ref, hn_ref,
                o_ref, st_ref, *, cs, nh, dk, dv, scale):
    @pl.when(pl.program_id(2) == 0)
    def _():
        st_ref[...] = jnp.zeros_like(st_ref)

    ga_hi, ga_lo = _split_bf16(ga_ref[...])
    w2_hi = w2h_ref[...]
    z = _dot(ga_hi, w2_hi) + _dot(ga_lo, w2_hi) + _dot(ga_hi, w2l_ref[...]) + ab_ref[...]
    log_a = -_softplus(-z) * (math.log2(math.e) / GLA_TAU)

    row = lax.broadcasted_iota(jnp.int32, (cs, cs), 0)
    col = lax.broadcasted_iota(jnp.int32, (cs, cs), 1)
    causal = col <= row
    tri = jnp.where(causal, 1.0, 0.0).astype(BF16)
    la_hi, la_lo = _split_bf16(log_a)
    b_all = _dot(tri, la_hi) + _dot(tri, la_lo)

    for h in range(nh):
        kc = slice(h * dk, (h + 1) * dk)
        vc = slice(h * dv, (h + 1) * dv)
        b = b_all[:, kc]
        b_last = b[cs - 1:cs, :]
        b_mid = b[cs // 2 - 1:cs // 2, :]
        q = q_ref[:, kc].astype(F32) * scale
        k = k_ref[:, kc].astype(F32)
        v = v_ref[:, vc]
        q_inter = (q * jnp.exp2(b)).astype(BF16)
        k_carry = (k * jnp.exp2(b_last - b)).astype(BF16)
        q_intra = (q * jnp.exp2(b - b_mid)).astype(BF16)
        k_intra = (k * jnp.exp2(b_mid - b)).astype(BF16)

        attn = jnp.where(causal, _dot_nt(q_intra, k_intra), 0.0).astype(BF16)
        st = st_ref[h]
        o = _dot_nt(q_inter, st.astype(BF16)) + _dot(attn, v)
        st_ref[h] = st * jnp.exp2(b_last) + _dot_tn(v, k_carry)

        ms = jnp.mean(o * o, axis=-1, keepdims=True)
        o = o * lax.rsqrt(ms + EPS) * hn_ref[...]
        gg = gg_ref[:, vc].astype(F32)
        o_ref[:, vc] = (o * (gg * _sigmoid(gg))).astype(o_ref.dtype)


def _gla(proj, ga, w2, a_b, hn_g, *, batch, seq, heads, dk, dv, col_q, col_k,
         col_v, col_g, cs, nh):
    nc = seq // cs
    wk, wv = nh * dk, nh * dv
    w2_hi, w2_lo = _split_bf16(jnp.pad(w2, ((0, LANES - w2.shape[0]), (0, 0))))
    kern = functools.partial(_gla_kernel, cs=cs, nh=nh, dk=dk, dv=dv, scale=dk ** -0.5)
    row = lambda b, c: b * nc + c
    return pl.pallas_call(
        kern,
        grid=(batch, heads // nh, nc),
        in_specs=[
            pl.BlockSpec((cs, wk), lambda b, h, c: (row(b, c), col_q // wk + h)),
            pl.BlockSpec((cs, wk), lambda b, h, c: (row(b, c), col_k // wk + h)),
            pl.BlockSpec((cs, wv), lambda b, h, c: (row(b, c), col_v // wv + h)),
            pl.BlockSpec((cs, wv), lambda b, h, c: (row(b, c), col_g // wv + h)),
            pl.BlockSpec((cs, LANES), lambda b, h, c: (row(b, c), 0)),
            pl.BlockSpec((LANES, wk), lambda b, h, c: (0, h)),
            pl.BlockSpec((LANES, wk), lambda b, h, c: (0, h)),
            pl.BlockSpec((1, wk), lambda b, h, c: (0, h)),
            pl.BlockSpec((1, dv), lambda b, h, c: (0, 0)),
        ],
        out_specs=pl.BlockSpec((cs, wv), lambda b, h, c: (row(b, c), h)),
        out_shape=jax.ShapeDtypeStruct((batch * seq, heads * dv), BF16),
        scratch_shapes=[pltpu.VMEM((nh, dv, dk), F32)],
        compiler_params=_params(("parallel", "parallel", "arbitrary")),
        name="gla",
    )(proj, proj, proj, proj, ga, w2_hi, w2_lo, a_b.reshape(1, -1), hn_g.reshape(1, -1))


def _sbproj_kernel(a_ref, w_ref, q_ref, k_ref, v_ref, sg_ref, p_ref, o_ref,
                   wb_ref, carry_ref, acc_ref, *, tq, nq, nh, d, scale2):
    i = pl.program_id(1)
    qi = (pl.program_id(0) * pl.num_programs(1) + i) % nq

    @pl.when(i == 0)
    def _():
        for r in range(0, wb_ref.shape[0], CAST_ROWS):
            rows = slice(r, r + CAST_ROWS)
            wb_ref[rows, :] = w_ref[rows, :].astype(BF16)

    r = lax.broadcasted_iota(jnp.int32, (2 * LANES, 2 * LANES), 0) % LANES
    c = lax.broadcasted_iota(jnp.int32, (2 * LANES, 2 * LANES), 1)
    cum_w = jnp.where((c >= LANES) | (r > c), 1.0, 0.0).astype(BF16)
    groups = tq // LANES

    piece = p_ref.shape[1] // 4

    def project(n):
        cols = slice(n * piece, (n + 1) * piece)
        p_ref[:, cols] = _dot_nt(a_ref[...], wb_ref[cols, :]).astype(p_ref.dtype)

    def sweep(key_start, mask, first=False, pieces=None):
        heads = [slice(h * d, (h + 1) * d) for h in range(nh)]
        u = [_dot_nt(q_ref[:, cols], k_ref[pl.ds(key_start, tq), cols]) * scale2
             for cols in heads]
        if pieces:
            project(pieces[0])
        sp, after = [], []
        for h in range(nh):
            sp_h = jnp.maximum(
                jnp.log2(1.0 + jnp.exp2(jnp.minimum(u[h], SB_EXP2_CLAMP))), u[h])
            sp_m = sp_h if mask is None else jnp.where(mask, sp_h, 0.0)
            carry = jnp.zeros((tq, LANES), F32) if first else carry_ref[h]
            after_h = [None] * groups
            for g in reversed(range(groups)):
                hi, lo = _split_bf16(sp_m[:, g * LANES:(g + 1) * LANES])
                ct = _dot(jnp.concatenate([hi, lo], axis=1), cum_w)
                after_h[g] = ct[:, :LANES] + carry
                carry = carry + ct[:, LANES:]
            carry_ref[h] = carry
            sp.append(sp_h)
            after.append(jnp.concatenate(after_h, axis=1))
        if pieces:
            project(pieces[1])
        for h in range(nh):
            w = jnp.exp2(u[h] - sp[h] - after[h])
            if mask is not None:
                w = jnp.where(mask, w, 0.0)
            pv = _dot(w.astype(BF16), v_ref[pl.ds(key_start, tq), heads[h]])
            acc_ref[h] = pv if first else acc_ref[h] + pv

    def least_carry():
        m = carry_ref[0]
        for h in range(1, nh):
            m = jnp.minimum(m, carry_ref[h])
        return jnp.min(m)

    t_idx = lax.broadcasted_iota(jnp.int32, (tq, tq), 0)
    s_idx = lax.broadcasted_iota(jnp.int32, (tq, tq), 1)
    sweep(pl.multiple_of(qi * tq, tq), s_idx < t_idx, first=True, pieces=(0, 1))
    has_prev = jnp.full((tq, tq), qi, jnp.int32) >= 1
    sweep(pl.multiple_of(jnp.maximum(qi - 1, 0) * tq, tq), has_prev, pieces=(2, 3))

    def cond(state):
        blk, low = state
        return jnp.logical_and(blk >= 0, low < SB_LOG2_CUTOFF)

    def body(state):
        blk, _ = state
        sweep(pl.multiple_of(blk * tq, tq), None)
        return blk - 1, least_carry()

    lax.while_loop(cond, body, (qi - 2, least_carry()))

    for h in range(nh):
        cols = slice(h * d, (h + 1) * d)
        sg = sg_ref[:, cols].astype(F32)
        o_ref[:, cols] = (acc_ref[h] * (sg * _sigmoid(sg))).astype(o_ref.dtype)


def _sbproj(a, w_t, proj, *, row0, width, tm, tn, batch, seq, heads, d,
            col_q, col_k, col_v, col_g, tq, nh):
    m, k = a.shape
    nq = seq // tq
    hgroups = heads // nh
    w = nh * d
    n_j, n_i = width // tn, m // tm
    assert n_j * n_i == batch * hgroups * nq, "one stick-breaking step per projection tile"

    def sb_pos(j, i):
        s = j * n_i + i
        return s // (hgroups * nq), (s // nq) % hgroups, s % nq

    def qmap(col):
        def index(j, i):
            b, g, qi = sb_pos(j, i)
            return b * nq + qi, col // w + g
        return index

    def kvmap(col):
        def index(j, i):
            b, g, _ = sb_pos(j, i)
            return b, col // w + g
        return index

    kern = functools.partial(_sbproj_kernel, tq=tq, nq=nq, nh=nh, d=d,
                             scale2=math.log2(math.e) / math.sqrt(d))
    return pl.pallas_call(
        kern,
        grid=(n_j, n_i),
        in_specs=[
            pl.BlockSpec((tm, k), lambda j, i: (i, 0)),
            pl.BlockSpec((pl.Element(tn), pl.Element(k)),
                         lambda j, i: ((row0 // SUBLANES + j * (tn // SUBLANES)) * SUBLANES, 0)),
            pl.BlockSpec((tq, w), qmap(col_q)),
            pl.BlockSpec((seq, w), kvmap(col_k)),
            pl.BlockSpec((seq, w), kvmap(col_v)),
            pl.BlockSpec((tq, w), qmap(col_g)),
        ],
        out_specs=[
            pl.BlockSpec((tm, tn), lambda j, i: (i, j)),
            pl.BlockSpec((tq, w), qmap(0)),
        ],
        out_shape=[
            jax.ShapeDtypeStruct((m, width), BF16),
            jax.ShapeDtypeStruct((batch * seq, heads * d), BF16),
        ],
        scratch_shapes=[pltpu.VMEM((tn, k), BF16),
                        pltpu.VMEM((nh, tq, LANES), F32), pltpu.VMEM((nh, tq, d), F32)],
        compiler_params=_params(("arbitrary", "arbitrary")),
        name="sbproj",
    )(a, w_t, proj, proj, proj, proj)


def _memattn_kernel(q_ref, k_ref, v_ref, o_ref, *, nh, d, scale):
    for h in range(nh):
        cols = slice(h * d, (h + 1) * d)
        s = _dot_nt(q_ref[:, cols], k_ref[:, cols]) * scale
        s = s - jnp.max(s, axis=-1, keepdims=True)
        p = jnp.exp(s)
        p = p / jnp.sum(p, axis=-1, keepdims=True)
        o_ref[:, cols] = _dot(p.astype(BF16), v_ref[:, cols]).astype(o_ref.dtype)


def _memattn(proj, mkv, *, batch, seq, heads, d, col_q, tq, nh):
    nq = seq // tq
    w = nh * d
    kern = functools.partial(_memattn_kernel, nh=nh, d=d, scale=1.0 / math.sqrt(d))
    return pl.pallas_call(
        kern,
        grid=(batch, heads // nh, nq),
        in_specs=[
            pl.BlockSpec((tq, w), lambda b, h, i: (b * nq + i, col_q // w + h)),
            pl.BlockSpec((N_MEM, w), lambda b, h, i: (b, h)),
            pl.BlockSpec((N_MEM, w), lambda b, h, i: (b, heads // nh + h)),
        ],
        out_specs=pl.BlockSpec((tq, w), lambda b, h, i: (b * nq + i, h)),
        out_shape=jax.ShapeDtypeStruct((batch * seq, heads * d), BF16),
        compiler_params=_params(("parallel", "parallel", "parallel")),
        name="memattn",
    )(proj, mkv, mkv)


def _merge_kernel(oa_ref, ob_ref, om_ref, wa_ref, wb_ref, wm_ref,
                  ga_ref, gb_ref, gm_ref, o_ref):
    acc = _sigmoid(ga_ref[...].astype(F32)) * _dot(oa_ref[...], wa_ref[...])
    acc += _sigmoid(gb_ref[...].astype(F32)) * _dot(ob_ref[...], wb_ref[...])
    acc += _sigmoid(gm_ref[...].astype(F32)) * _dot(om_ref[...], wm_ref[...])
    o_ref[...] = acc.astype(o_ref.dtype)


def _merge(o_a, o_b, o_m, w_a, w_b, w_m, proj, *, col_gates, tm, tn):
    m, d = o_a.shape
    gate_blk = col_gates // tn
    act = lambda: pl.BlockSpec((tm, d), lambda i, j: (i, 0))
    wgt = lambda: pl.BlockSpec((d, tn), lambda i, j: (0, j))
    gate = lambda n: pl.BlockSpec((tm, tn), lambda i, j: (i, gate_blk + n * (d // tn) + j))
    return pl.pallas_call(
        _merge_kernel,
        grid=(m // tm, d // tn),
        in_specs=[act(), act(), act(), wgt(), wgt(), wgt(), gate(0), gate(1), gate(2)],
        out_specs=pl.BlockSpec((tm, tn), lambda i, j: (i, j)),
        out_shape=jax.ShapeDtypeStruct((m, d), BF16),
        compiler_params=_params(("parallel", "parallel")),
        name="merge",
    )(o_a, o_b, o_m, w_a, w_b, w_m, proj, proj, proj)


def _out_kernel(m_ref, w_ref, g_ref, x_ref, o_ref):
    y = _dot(m_ref[...], w_ref[...])
    ms = jnp.mean(y * y, axis=-1, keepdims=True)
    o_ref[...] = x_ref[...] + y * lax.rsqrt(ms + EPS) * g_ref[...]


def _out(merged, w_out, g, x2d, tm):
    m, d = x2d.shape
    return pl.pallas_call(
        _out_kernel,
        grid=(m // tm,),
        in_specs=[
            pl.BlockSpec((tm, d), lambda i: (i, 0)),
            pl.BlockSpec((d, d), lambda i: (0, 0)),
            pl.BlockSpec((1, d), lambda i: (0, 0)),
            pl.BlockSpec((tm, d), lambda i: (i, 0)),
        ],
        out_specs=pl.BlockSpec((tm, d), lambda i: (i, 0)),
        out_shape=jax.ShapeDtypeStruct((m, d), F32),
        compiler_params=_params(("parallel",)),
        name="outproj",
    )(merged, w_out, g.reshape(1, d), x2d)


def kernel(x, mem, norm_pre_g, norm_post_g, norm_mem_g, w_in, gla_a_w2, gla_a_b,
           gla_head_norm_g, w_mem_kv, w_proj_gla, w_proj_sb, w_proj_mem, w_out):
    batch, seq, d = x.shape
    dk_total, dv_total = d // 2, d
    dk, dv = dk_total // GLA_HEADS, dv_total // GLA_HEADS
    sb_heads = d // SB_HEAD_DIM
    mem_d = d // MEM_HEADS

    c_ga = 2 * dk_total + 2 * dv_total

    x2d = x.reshape(batch * seq, d)
    mem2d = mem.reshape(batch * N_MEM, d)

    mkv = _memkv(mem2d, norm_mem_g, w_mem_kv, tn=512)
    w_in_t = w_in.T
    h, ga = _prenorm(x2d, norm_pre_g, w_in_t, c_ga, tm=512)
    c_sb = c_ga + GLA_RANK
    proj_a = _inproj(h, w_in_t, row0=0, width=c_ga, tm=2048, tn=1024, name="inproj_a")
    proj_s = _inproj(h, w_in_t, row0=c_sb, width=4 * d, tm=2048, tn=1024, name="inproj_s")

    o_a = _gla(proj_a, ga, gla_a_w2, gla_a_b, gla_head_norm_g, batch=batch, seq=seq,
               heads=GLA_HEADS, dk=dk, dv=dv, col_q=0, col_k=dk_total, col_v=2 * dk_total,
               col_g=2 * dk_total + dv_total, cs=256, nh=4)
    proj_m, o_b = _sbproj(h, w_in_t, proj_s, row0=c_sb + 4 * d, width=(1 + N_BRANCHES) * d,
                          tm=512, tn=1024, batch=batch, seq=seq, heads=sb_heads, d=SB_HEAD_DIM,
                          col_q=0, col_k=d, col_v=2 * d, col_g=3 * d, tq=256, nh=4)
    o_m = _memattn(proj_m, mkv, batch=batch, seq=seq, heads=MEM_HEADS, d=mem_d,
                   col_q=0, tq=256, nh=4)

    merged = _merge(o_a, o_b, o_m, w_proj_gla.astype(BF16), w_proj_sb.astype(BF16),
                    w_proj_mem.astype(BF16), proj_m, col_gates=d, tm=1024, tn=512)
    out = _out(merged, w_out.astype(BF16), norm_post_g, x2d, tm=512)
    return out.reshape(batch, seq, d)
```

```python
import functools
import math

import jax
import jax.numpy as jnp
from jax import lax
from jax.experimental import pallas as pl
from jax.experimental.pallas import tpu as pltpu

F32 = jnp.float32
BF16 = jnp.bfloat16

EPS = 1e-6
N_MEM = 256
GLA_HEADS = 4
GLA_RANK = 16
GLA_TAU = 16.0
SB_HEAD_DIM = 128
MEM_HEADS = 4
N_BRANCHES = 3

LANES = 128
SUBLANES = 8
VMEM_LIMIT = 56 * 1024 * 1024
SB_LOG2_CUTOFF = 90.0 * math.log2(math.e)
SB_EXP2_CLAMP = 126.0


def _params(sem):
    return pltpu.CompilerParams(dimension_semantics=sem, vmem_limit_bytes=VMEM_LIMIT)


def _split_bf16(x):
    hi = x.astype(BF16)
    lo = (x - hi.astype(F32)).astype(BF16)
    return hi, lo


def _dot(a, b):
    return jnp.dot(a, b, preferred_element_type=F32)


def _dot_nt(a, b):
    return lax.dot_general(a, b, (((1,), (1,)), ((), ())), preferred_element_type=F32)


def _dot_tn(a, b):
    return lax.dot_general(a, b, (((0,), (0,)), ((), ())), preferred_element_type=F32)


def _softplus(z):
    return jnp.maximum(z, 0.0) + jnp.log(1.0 + jnp.exp(-jnp.abs(z)))


def _sigmoid(z):
    return 1.0 / (1.0 + jnp.exp(-z))


def _prenorm_kernel(x_ref, g_ref, wga_ref, h_ref, ga_ref):
    x = x_ref[...]
    ms = jnp.mean(x * x, axis=-1, keepdims=True)
    h = (x * lax.rsqrt(ms + EPS) * g_ref[...]).astype(BF16)
    h_ref[...] = h
    ga_ref[...] = _dot_nt(h, wga_ref[...].astype(BF16))


def _prenorm(x2d, g, w_in_t, col_ga, tm):
    m, d = x2d.shape
    return pl.pallas_call(
        _prenorm_kernel,
        grid=(m // tm,),
        in_specs=[
            pl.BlockSpec((tm, d), lambda i: (i, 0)),
            pl.BlockSpec((1, d), lambda i: (0, 0)),
            pl.BlockSpec((LANES, d), lambda i: (col_ga // LANES, 0)),
        ],
        out_specs=[
            pl.BlockSpec((tm, d), lambda i: (i, 0)),
            pl.BlockSpec((tm, LANES), lambda i: (i, 0)),
        ],
        out_shape=[
            jax.ShapeDtypeStruct((m, d), BF16),
            jax.ShapeDtypeStruct((m, LANES), F32),
        ],
        compiler_params=_params(("parallel",)),
        name="prenorm",
    )(x2d, g.reshape(1, d), w_in_t)


CAST_ROWS = 128


def _inproj_kernel(a_ref, w_ref, o_ref, wb_ref):
    @pl.when(pl.program_id(1) == 0)
    def _():
        for r in range(0, wb_ref.shape[0], CAST_ROWS):
            rows = slice(r, r + CAST_ROWS)
            wb_ref[rows, :] = w_ref[rows, :].astype(BF16)

    o_ref[...] = _dot_nt(a_ref[...], wb_ref[...]).astype(o_ref.dtype)


def _inproj(a, w_t, *, row0, width, tm, tn, name):
    m, k = a.shape
    return pl.pallas_call(
        _inproj_kernel,
        grid=(width // tn, m // tm),
        in_specs=[
            pl.BlockSpec((tm, k), lambda j, i: (i, 0)),
            pl.BlockSpec((pl.Element(tn), pl.Element(k)),
                         lambda j, i: ((row0 // SUBLANES + j * (tn // SUBLANES)) * SUBLANES, 0)),
        ],
        out_specs=pl.BlockSpec((tm, tn), lambda j, i: (i, j)),
        out_shape=jax.ShapeDtypeStruct((m, width), BF16),
        scratch_shapes=[pltpu.VMEM((tn, k), BF16)],
        compiler_params=_params(("parallel", "arbitrary")),
        name=name,
    )(a, w_t)


def _memkv_kernel(m_ref, g_ref, w_ref, o_ref):
    x = m_ref[...]
    ms = jnp.mean(x * x, axis=-1, keepdims=True)
    h = (x * lax.rsqrt(ms + EPS) * g_ref[...]).astype(BF16)
    o_ref[...] = _dot(h, w_ref[...].astype(BF16)).astype(o_ref.dtype)


def _memkv(mem2d, g, w, tn):
    m, d = mem2d.shape
    n = w.shape[1]
    return pl.pallas_call(
        _memkv_kernel,
        grid=(n // tn,),
        in_specs=[
            pl.BlockSpec((m, d), lambda j: (0, 0)),
            pl.BlockSpec((1, d), lambda j: (0, 0)),
            pl.BlockSpec((d, tn), lambda j: (0, j)),
        ],
        out_specs=pl.BlockSpec((m, tn), lambda j: (0, j)),
        out_shape=jax.ShapeDtypeStruct((m, n), BF16),
        compiler_params=_params(("parallel",)),
        name="memkv",
    )(mem2d, g.reshape(1, d), w)


def _glaproj_kernel(a_ref, w_ref, q_ref, k_ref, v_ref, gg_ref, ga_ref, w2h_ref, w2l_ref,
                    ab_ref, hn_ref, p_ref, o_ref, wb_ref, st_ref, *, cs, nc, nh, dk, dv, scale):
    i = pl.program_id(1)
    chunk = (pl.program_id(0) * pl.num_programs(1) + i) % nc

    @pl.when(i == 0)
    def _():
        for r in range(0, wb_ref.shape[0], CAST_ROWS):
            rows = slice(r, r + CAST_ROWS)
            wb_ref[rows, :] = w_ref[rows, :].astype(BF16)

    @pl.when(chunk == 0)
    def _():
        st_ref[...] = jnp.zeros_like(st_ref)

    piece = p_ref.shape[1] // 4

    def project(n):
        cols = slice(n * piece, (n + 1) * piece)
        p_ref[:, cols] = _dot_nt(a_ref[...], wb_ref[cols, :]).astype(p_ref.dtype)

    ga_hi, ga_lo = _split_bf16(ga_ref[...])
    w2_hi = w2h_ref[...]
    z = _dot(ga_hi, w2_hi) + _dot(ga_lo, w2_hi) + _dot(ga_hi, w2l_ref[...]) + ab_ref[...]
    project(0)
    log_a = -_softplus(-z) * (math.log2(math.e) / GLA_TAU)

    row = lax.broadcasted_iota(jnp.int32, (cs, cs), 0)
    col = lax.broadcasted_iota(jnp.int32, (cs, cs), 1)
    causal = col <= row
    tri = jnp.where(causal, 1.0, 0.0).astype(BF16)
    la_hi, la_lo = _split_bf16(log_a)
    b_all = _dot(tri, la_hi) + _dot(tri, la_lo)
    project(1)

    kcols = [slice(h * dk, (h + 1) * dk) for h in range(nh)]
    vcols = [slice(h * dv, (h + 1) * dv) for h in range(nh)]
    attn, o, decay, kv = [], [], [], []
    for h in range(nh):
        b = b_all[:, kcols[h]]
        b_last = b[cs - 1:cs, :]
        b_mid = b[cs // 2 - 1:cs // 2, :]
        q = q_ref[:, kcols[h]].astype(F32) * scale
        k = k_ref[:, kcols[h]].astype(F32)
        q_inter = (q * jnp.exp2(b)).astype(BF16)
        k_carry = (k * jnp.exp2(b_last - b)).astype(BF16)
        q_intra = (q * jnp.exp2(b - b_mid)).astype(BF16)
        k_intra = (k * jnp.exp2(b_mid - b)).astype(BF16)
        attn.append(_dot_nt(q_intra, k_intra))
        o.append(_dot_nt(q_inter, st_ref[h].astype(BF16)))
        kv.append(_dot_tn(v_ref[:, vcols[h]], k_carry))
        decay.append(jnp.exp2(b_last))
    project(2)

    for h in range(nh):
        st_ref[h] = st_ref[h] * decay[h] + kv[h]
        o[h] = o[h] + _dot(jnp.where(causal, attn[h], 0.0).astype(BF16), v_ref[:, vcols[h]])
    project(3)

    for h in range(nh):
        ms = jnp.mean(o[h] * o[h], axis=-1, keepdims=True)
        on = o[h] * lax.rsqrt(ms + EPS) * hn_ref[...]
        gg = gg_ref[:, vcols[h]].astype(F32)
        o_ref[:, vcols[h]] = (on * (gg * _sigmoid(gg))).astype(o_ref.dtype)


def _glaproj(a, w_t, proj, ga, w2, a_b, hn_g, *, row0, width, tm, tn, batch, seq, heads,
             dk, dv, col_q, col_k, col_v, col_g, cs):
    m, k = a.shape
    nc = seq // cs
    wk, wv = heads * dk, heads * dv
    n_j, n_i = width // tn, m // tm
    assert n_j * n_i == batch * nc, "one GLA chunk per projection tile"
    w2_hi, w2_lo = _split_bf16(jnp.pad(w2, ((0, LANES - w2.shape[0]), (0, 0))))
    kern = functools.partial(_glaproj_kernel, cs=cs, nc=nc, nh=heads, dk=dk, dv=dv,
                             scale=dk ** -0.5)
    tok = lambda col, wd: pl.BlockSpec((cs, wd), lambda j, i: (j * n_i + i, col // wd))
    const = lambda shape: pl.BlockSpec(shape, lambda j, i: (0, 0))
    return pl.pallas_call(
        kern,
        grid=(n_j, n_i),
        in_specs=[
            pl.BlockSpec((tm, k), lambda j, i: (i, 0)),
            pl.BlockSpec((pl.Element(tn), pl.Element(k)),
                         lambda j, i: ((row0 // SUBLANES + j * (tn // SUBLANES)) * SUBLANES, 0)),
            tok(col_q, wk), tok(col_k, wk), tok(col_v, wv), tok(col_g, wv), tok(0, LANES),
            const((LANES, wk)), const((LANES, wk)), const((1, wk)), const((1, dv)),
        ],
        out_specs=[
            pl.BlockSpec((tm, tn), lambda j, i: (i, j)),
            tok(0, wv),
        ],
        out_shape=[
            jax.ShapeDtypeStruct((m, width), BF16),
            jax.ShapeDtypeStruct((batch * seq, wv), BF16),
        ],
        scratch_shapes=[pltpu.VMEM((tn, k), BF16), pltpu.VMEM((heads, dv, dk), F32)],
        compiler_params=_params(("arbitrary", "arbitrary")),
        name="glaproj",
    )(a, w_t, proj, proj, proj, proj, ga, w2_hi, w2_lo, a_b.reshape(1, -1), hn_g.reshape(1, -1))


def _sbproj_kernel(a_ref, w_ref, q_ref, k_ref, v_ref, sg_ref, p_ref, o_ref,
                   wb_ref, carry_ref, acc_ref, *, tq, nq, nh, d, scale2):
    i = pl.program_id(1)
    qi = (pl.program_id(0) * pl.num_programs(1) + i) % nq

    @pl.when(i == 0)
    def _():
        for r in range(0, wb_ref.shape[0], CAST_ROWS):
            rows = slice(r, r + CAST_ROWS)
            wb_ref[rows, :] = w_ref[rows, :].astype(BF16)

    r = lax.broadcasted_iota(jnp.int32, (2 * LANES, 2 * LANES), 0) % LANES
    c = lax.broadcasted_iota(jnp.int32, (2 * LANES, 2 * LANES), 1)
    cum_w = jnp.where((c >= LANES) | (r > c), 1.0, 0.0).astype(BF16)
    groups = tq // LANES

    piece = p_ref.shape[1] // 4

    def project(n):
        cols = slice(n * piece, (n + 1) * piece)
        p_ref[:, cols] = _dot_nt(a_ref[...], wb_ref[cols, :]).astype(p_ref.dtype)

    def sweep(key_start, mask, first=False, pieces=None):
        heads = [slice(h * d, (h + 1) * d) for h in range(nh)]
        u = [_dot_nt(q_ref[:, cols], k_ref[pl.ds(key_start, tq), cols]) * scale2
             for cols in heads]
        if pieces:
            project(pieces[0])
        sp, after = [], []
        for h in range(nh):
            sp_h = jnp.maximum(
                jnp.log2(1.0 + jnp.exp2(jnp.minimum(u[h], SB_EXP2_CLAMP))), u[h])
            sp_m = sp_h if mask is None else jnp.where(mask, sp_h, 0.0)
            carry = jnp.zeros((tq, LANES), F32) if first else carry_ref[h]
            after_h = [None] * groups
            for g in reversed(range(groups)):
                hi, lo = _split_bf16(sp_m[:, g * LANES:(g + 1) * LANES])
                ct = _dot(jnp.concatenate([hi, lo], axis=1), cum_w)
                after_h[g] = ct[:, :LANES] + carry
                carry = carry + ct[:, LANES:]
            carry_ref[h] = carry
            sp.append(sp_h)
            after.append(jnp.concatenate(after_h, axis=1))
        if pieces:
            project(pieces[1])
        for h in range(nh):
            w = jnp.exp2(u[h] - sp[h] - after[h])
            if mask is not None:
                w = jnp.where(mask, w, 0.0)
            pv = _dot(w.astype(BF16), v_ref[pl.ds(key_start, tq), heads[h]])
            acc_ref[h] = pv if first else acc_ref[h] + pv

    def least_carry():
        m = carry_ref[0]
        for h in range(1, nh):
            m = jnp.minimum(m, carry_ref[h])
        return jnp.min(m)

    t_idx = lax.broadcasted_iota(jnp.int32, (tq, tq), 0)
    s_idx = lax.broadcasted_iota(jnp.int32, (tq, tq), 1)
    sweep(pl.multiple_of(qi * tq, tq), s_idx < t_idx, first=True, pieces=(0, 1))
    has_prev = jnp.full((tq, tq), qi, jnp.int32) >= 1
    sweep(pl.multiple_of(jnp.maximum(qi - 1, 0) * tq, tq), has_prev, pieces=(2, 3))

    def cond(state):
        blk, low = state
        return jnp.logical_and(blk >= 0, low < SB_LOG2_CUTOFF)

    def body(state):
        blk, _ = state
        sweep(pl.multiple_of(blk * tq, tq), None)
        return blk - 1, least_carry()

    lax.while_loop(cond, body, (qi - 2, least_carry()))

    for h in range(nh):
        cols = slice(h * d, (h + 1) * d)
        sg = sg_ref[:, cols].astype(F32)
        o_ref[:, cols] = (acc_ref[h] * (sg * _sigmoid(sg))).astype(o_ref.dtype)


def _sbproj(a, w_t, proj, *, row0, width, tm, tn, batch, seq, heads, d,
            col_q, col_k, col_v, col_g, tq, nh):
    m, k = a.shape
    nq = seq // tq
    hgroups = heads // nh
    w = nh * d
    n_j, n_i = width // tn, m // tm
    assert n_j * n_i == batch * hgroups * nq, "one stick-breaking step per projection tile"

    def sb_pos(j, i):
        s = j * n_i + i
        return s // (hgroups * nq), (s // nq) % hgroups, s % nq

    def qmap(col):
        def index(j, i):
            b, g, qi = sb_pos(j, i)
            return b * nq + qi, col // w + g
        return index

    def kvmap(col):
        def index(j, i):
            b, g, _ = sb_pos(j, i)
            return b, col // w + g
        return index

    kern = functools.partial(_sbproj_kernel, tq=tq, nq=nq, nh=nh, d=d,
                             scale2=math.log2(math.e) / math.sqrt(d))
    return pl.pallas_call(
        kern,
        grid=(n_j, n_i),
        in_specs=[
            pl.BlockSpec((tm, k), lambda j, i: (i, 0)),
            pl.BlockSpec((pl.Element(tn), pl.Element(k)),
                         lambda j, i: ((row0 // SUBLANES + j * (tn // SUBLANES)) * SUBLANES, 0)),
            pl.BlockSpec((tq, w), qmap(col_q)),
            pl.BlockSpec((seq, w), kvmap(col_k)),
            pl.BlockSpec((seq, w), kvmap(col_v)),
            pl.BlockSpec((tq, w), qmap(col_g)),
        ],
        out_specs=[
            pl.BlockSpec((tm, tn), lambda j, i: (i, j)),
            pl.BlockSpec((tq, w), qmap(0)),
        ],
        out_shape=[
            jax.ShapeDtypeStruct((m, width), BF16),
            jax.ShapeDtypeStruct((batch * seq, heads * d), BF16),
        ],
        scratch_shapes=[pltpu.VMEM((tn, k), BF16),
                        pltpu.VMEM((nh, tq, LANES), F32), pltpu.VMEM((nh, tq, d), F32)],
        compiler_params=_params(("arbitrary", "arbitrary")),
        name="sbproj",
    )(a, w_t, proj, proj, proj, proj)


def _memattn_kernel(q_ref, k_ref, v_ref, o_ref, *, nh, d, scale):
    for h in range(nh):
        cols = slice(h * d, (h + 1) * d)
        s = _dot_nt(q_ref[:, cols], k_ref[:, cols]) * scale
        s = s - jnp.max(s, axis=-1, keepdims=True)
        p = jnp.exp(s)
        p = p / jnp.sum(p, axis=-1, keepdims=True)
        o_ref[:, cols] = _dot(p.astype(BF16), v_ref[:, cols]).astype(o_ref.dtype)


def _memattn(proj, mkv, *, batch, seq, heads, d, col_q, tq, nh):
    nq = seq // tq
    w = nh * d
    kern = functools.partial(_memattn_kernel, nh=nh, d=d, scale=1.0 / math.sqrt(d))
    return pl.pallas_call(
        kern,
        grid=(batch, heads // nh, nq),
        in_specs=[
            pl.BlockSpec((tq, w), lambda b, h, i: (b * nq + i, col_q // w + h)),
            pl.BlockSpec((N_MEM, w), lambda b, h, i: (b, h)),
            pl.BlockSpec((N_MEM, w), lambda b, h, i: (b, heads // nh + h)),
        ],
        out_specs=pl.BlockSpec((tq, w), lambda b, h, i: (b * nq + i, h)),
        out_shape=jax.ShapeDtypeStruct((batch * seq, heads * d), BF16),
        compiler_params=_params(("parallel", "parallel", "parallel")),
        name="memattn",
    )(proj, mkv, mkv)


def _merge_kernel(oa_ref, ob_ref, om_ref, wa_ref, wb_ref, wm_ref,
                  ga_ref, gb_ref, gm_ref, o_ref):
    acc = _sigmoid(ga_ref[...].astype(F32)) * _dot(oa_ref[...], wa_ref[...])
    acc += _sigmoid(gb_ref[...].astype(F32)) * _dot(ob_ref[...], wb_ref[...])
    acc += _sigmoid(gm_ref[...].astype(F32)) * _dot(om_ref[...], wm_ref[...])
    o_ref[...] = acc.astype(o_ref.dtype)


def _merge(o_a, o_b, o_m, w_a, w_b, w_m, proj, *, col_gates, tm, tn):
    m, d = o_a.shape
    gate_blk = col_gates // tn
    act = lambda: pl.BlockSpec((tm, d), lambda i, j: (i, 0))
    wgt = lambda: pl.BlockSpec((d, tn), lambda i, j: (0, j))
    gate = lambda n: pl.BlockSpec((tm, tn), lambda i, j: (i, gate_blk + n * (d // tn) + j))
    return pl.pallas_call(
        _merge_kernel,
        grid=(m // tm, d // tn),
        in_specs=[act(), act(), act(), wgt(), wgt(), wgt(), gate(0), gate(1), gate(2)],
        out_specs=pl.BlockSpec((tm, tn), lambda i, j: (i, j)),
        out_shape=jax.ShapeDtypeStruct((m, d), BF16),
        compiler_params=_params(("parallel", "parallel")),
        name="merge",
    )(o_a, o_b, o_m, w_a, w_b, w_m, proj, proj, proj)


def _out_kernel(m_ref, w_ref, g_ref, x_ref, o_ref):
    y = _dot(m_ref[...], w_ref[...])
    ms = jnp.mean(y * y, axis=-1, keepdims=True)
    o_ref[...] = x_ref[...] + y * lax.rsqrt(ms + EPS) * g_ref[...]


def _out(merged, w_out, g, x2d, tm):
    m, d = x2d.shape
    return pl.pallas_call(
        _out_kernel,
        grid=(m // tm,),
        in_specs=[
            pl.BlockSpec((tm, d), lambda i: (i, 0)),
            pl.BlockSpec((d, d), lambda i: (0, 0)),
            pl.BlockSpec((1, d), lambda i: (0, 0)),
            pl.BlockSpec((tm, d), lambda i: (i, 0)),
        ],
        out_specs=pl.BlockSpec((tm, d), lambda i: (i, 0)),
        out_shape=jax.ShapeDtypeStruct((m, d), F32),
        compiler_params=_params(("parallel",)),
        name="outproj",
    )(merged, w_out, g.reshape(1, d), x2d)


def kernel(x, mem, norm_pre_g, norm_post_g, norm_mem_g, w_in, gla_a_w2, gla_a_b,
           gla_head_norm_g, w_mem_kv, w_proj_gla, w_proj_sb, w_proj_mem, w_out):
    batch, seq, d = x.shape
    dk_total, dv_total = d // 2, d
    dk, dv = dk_total // GLA_HEADS, dv_total // GLA_HEADS
    sb_heads = d // SB_HEAD_DIM
    mem_d = d // MEM_HEADS

    c_ga = 2 * dk_total + 2 * dv_total

    x2d = x.reshape(batch * seq, d)
    mem2d = mem.reshape(batch * N_MEM, d)

    mkv = _memkv(mem2d, norm_mem_g, w_mem_kv, tn=512)
    w_in_t = w_in.T
    h, ga = _prenorm(x2d, norm_pre_g, w_in_t, c_ga, tm=512)
    c_sb = c_ga + GLA_RANK
    proj_a = _inproj(h, w_in_t, row0=0, width=c_ga, tm=2048, tn=1024, name="inproj_a")
    proj_s, o_a = _glaproj(h, w_in_t, proj_a, ga, gla_a_w2, gla_a_b, gla_head_norm_g,
                           row0=c_sb, width=4 * d, tm=(4 * d // 1024) * 128, tn=1024,
                           batch=batch, seq=seq, heads=GLA_HEADS, dk=dk, dv=dv, col_q=0,
                           col_k=dk_total, col_v=2 * dk_total, col_g=2 * dk_total + dv_total,
                           cs=128)
    proj_m, o_b = _sbproj(h, w_in_t, proj_s, row0=c_sb + 4 * d, width=(1 + N_BRANCHES) * d,
                          tm=512, tn=1024, batch=batch, seq=seq, heads=sb_heads, d=SB_HEAD_DIM,
                          col_q=0, col_k=d, col_v=2 * d, col_g=3 * d, tq=256, nh=4)
    o_m = _memattn(proj_m, mkv, batch=batch, seq=seq, heads=MEM_HEADS, d=mem_d,
                   col_q=0, tq=256, nh=4)

    merged = _merge(o_a, o_b, o_m, w_proj_gla.astype(BF16), w_proj_sb.astype(BF16),
                    w_proj_mem.astype(BF16), proj_m, col_gates=d, tm=1024, tn=512)
    out = _out(merged, w_out.astype(BF16), norm_post_g, x2d, tm=512)
    return out.reshape(batch, seq, d)
```

```python
import functools
import math

import jax
import jax.numpy as jnp
from jax import lax
from jax.experimental import pallas as pl
from jax.experimental.pallas import tpu as pltpu

F32 = jnp.float32
BF16 = jnp.bfloat16

EPS = 1e-6
N_MEM = 256
GLA_HEADS = 4
GLA_RANK = 16
GLA_TAU = 16.0
SB_HEAD_DIM = 128
MEM_HEADS = 4
N_BRANCHES = 3

LANES = 128
SUBLANES = 8
VMEM_LIMIT = 56 * 1024 * 1024
SB_LOG2_CUTOFF = 90.0 * math.log2(math.e)
SB_EXP2_CLAMP = 126.0


def _params(sem):
    return pltpu.CompilerParams(dimension_semantics=sem, vmem_limit_bytes=VMEM_LIMIT)


def _split_bf16(x):
    hi = x.astype(BF16)
    lo = (x - hi.astype(F32)).astype(BF16)
    return hi, lo


def _dot(a, b):
    return jnp.dot(a, b, preferred_element_type=F32)


def _dot_nt(a, b):
    return lax.dot_general(a, b, (((1,), (1,)), ((), ())), preferred_element_type=F32)


def _dot_tn(a, b):
    return lax.dot_general(a, b, (((0,), (0,)), ((), ())), preferred_element_type=F32)


def _softplus(z):
    return jnp.maximum(z, 0.0) + jnp.log(1.0 + jnp.exp(-jnp.abs(z)))


def _sigmoid(z):
    return 1.0 / (1.0 + jnp.exp(-z))


def _prenorm_kernel(x_ref, g_ref, wga_ref, h_ref, ga_ref):
    x = x_ref[...]
    ms = jnp.mean(x * x, axis=-1, keepdims=True)
    h = (x * lax.rsqrt(ms + EPS) * g_ref[...]).astype(BF16)
    h_ref[...] = h
    ga_ref[...] = _dot_nt(h, wga_ref[...].astype(BF16))


def _prenorm(x2d, g, w_in_t, col_ga, tm):
    m, d = x2d.shape
    return pl.pallas_call(
        _prenorm_kernel,
        grid=(m // tm,),
        in_specs=[
            pl.BlockSpec((tm, d), lambda i: (i, 0)),
            pl.BlockSpec((1, d), lambda i: (0, 0)),
            pl.BlockSpec((LANES, d), lambda i: (col_ga // LANES, 0)),
        ],
        out_specs=[
            pl.BlockSpec((tm, d), lambda i: (i, 0)),
            pl.BlockSpec((tm, LANES), lambda i: (i, 0)),
        ],
        out_shape=[
            jax.ShapeDtypeStruct((m, d), BF16),
            jax.ShapeDtypeStruct((m, LANES), F32),
        ],
        compiler_params=_params(("parallel",)),
        name="prenorm",
    )(x2d, g.reshape(1, d), w_in_t)


CAST_ROWS = 128


def _inproj_kernel(a_ref, w_ref, o_ref, wb_ref):
    @pl.when(pl.program_id(1) == 0)
    def _():
        for r in range(0, wb_ref.shape[0], CAST_ROWS):
            rows = slice(r, r + CAST_ROWS)
            wb_ref[rows, :] = w_ref[rows, :].astype(BF16)

    o_ref[...] = _dot_nt(a_ref[...], wb_ref[...]).astype(o_ref.dtype)


def _inproj(a, w_t, *, row0, width, tm, tn, name):
    m, k = a.shape
    return pl.pallas_call(
        _inproj_kernel,
        grid=(width // tn, m // tm),
        in_specs=[
            pl.BlockSpec((tm, k), lambda j, i: (i, 0)),
            pl.BlockSpec((pl.Element(tn), pl.Element(k)),
                         lambda j, i: ((row0 // SUBLANES + j * (tn // SUBLANES)) * SUBLANES, 0)),
        ],
        out_specs=pl.BlockSpec((tm, tn), lambda j, i: (i, j)),
        out_shape=jax.ShapeDtypeStruct((m, width), BF16),
        scratch_shapes=[pltpu.VMEM((tn, k), BF16)],
        compiler_params=_params(("parallel", "arbitrary")),
        name=name,
    )(a, w_t)


def _memkv_kernel(m_ref, g_ref, w_ref, o_ref):
    x = m_ref[...]
    ms = jnp.mean(x * x, axis=-1, keepdims=True)
    h = (x * lax.rsqrt(ms + EPS) * g_ref[...]).astype(BF16)
    o_ref[...] = _dot(h, w_ref[...].astype(BF16)).astype(o_ref.dtype)


def _memkv(mem2d, g, w, tn):
    m, d = mem2d.shape
    n = w.shape[1]
    return pl.pallas_call(
        _memkv_kernel,
        grid=(n // tn,),
        in_specs=[
            pl.BlockSpec((m, d), lambda j: (0, 0)),
            pl.BlockSpec((1, d), lambda j: (0, 0)),
            pl.BlockSpec((d, tn), lambda j: (0, j)),
        ],
        out_specs=pl.BlockSpec((m, tn), lambda j: (0, j)),
        out_shape=jax.ShapeDtypeStruct((m, n), BF16),
        compiler_params=_params(("parallel",)),
        name="memkv",
    )(mem2d, g.reshape(1, d), w)


def _gla_kernel(q_ref, k_ref, v_ref, gg_ref, ga_ref, w2h_ref, w2l_ref, ab_ref, hn_ref,
                o_ref, st_ref, *, cs, nh, dk, dv, scale):
    @pl.when(pl.program_id(2) == 0)
    def _():
        st_ref[...] = jnp.zeros_like(st_ref)

    ga_hi, ga_lo = _split_bf16(ga_ref[...])
    w2_hi = w2h_ref[...]
    z = _dot(ga_hi, w2_hi) + _dot(ga_lo, w2_hi) + _dot(ga_hi, w2l_ref[...]) + ab_ref[...]
    log_a = -_softplus(-z) * (math.log2(math.e) / GLA_TAU)

    row = lax.broadcasted_iota(jnp.int32, (cs, cs), 0)
    col = lax.broadcasted_iota(jnp.int32, (cs, cs), 1)
    causal = col <= row
    tri = jnp.where(causal, 1.0, 0.0).astype(BF16)
    la_hi, la_lo = _split_bf16(log_a)
    b_all = _dot(tri, la_hi) + _dot(tri, la_lo)

    kcols = [slice(h * dk, (h + 1) * dk) for h in range(nh)]
    vcols = [slice(h * dv, (h + 1) * dv) for h in range(nh)]
    for h in range(nh):
        b = b_all[:, kcols[h]]
        b_last = b[cs - 1:cs, :]
        b_mid = b[cs // 2 - 1:cs // 2, :]
        q = q_ref[:, kcols[h]].astype(F32) * scale
        k = k_ref[:, kcols[h]].astype(F32)
        v = v_ref[:, vcols[h]]
        q_inter = (q * jnp.exp2(b)).astype(BF16)
        k_carry = (k * jnp.exp2(b_last - b)).astype(BF16)
        q_intra = (q * jnp.exp2(b - b_mid)).astype(BF16)
        k_intra = (k * jnp.exp2(b_mid - b)).astype(BF16)

        attn = jnp.where(causal, _dot_nt(q_intra, k_intra), 0.0).astype(BF16)
        st = st_ref[h]
        o = _dot_nt(q_inter, st.astype(BF16)) + _dot(attn, v)
        st_ref[h] = st * jnp.exp2(b_last) + _dot_tn(v, k_carry)

        ms = jnp.mean(o * o, axis=-1, keepdims=True)
        o = o * lax.rsqrt(ms + EPS) * hn_ref[...]
        gg = gg_ref[:, vcols[h]].astype(F32)
        o_ref[:, vcols[h]] = (o * (gg * _sigmoid(gg))).astype(o_ref.dtype)


def _gla(proj, ga, w2, a_b, hn_g, *, batch, seq, heads, dk, dv, col_q, col_k,
         col_v, col_g, cs, nh):
    nc = seq // cs
    wk, wv = nh * dk, nh * dv
    w2_hi, w2_lo = _split_bf16(jnp.pad(w2, ((0, LANES - w2.shape[0]), (0, 0))))
    kern = functools.partial(_gla_kernel, cs=cs, nh=nh, dk=dk, dv=dv, scale=dk ** -0.5)
    row = lambda b, c: b * nc + c
    return pl.pallas_call(
        kern,
        grid=(batch, heads // nh, nc),
        in_specs=[
            pl.BlockSpec((cs, wk), lambda b, h, c: (row(b, c), col_q // wk + h)),
            pl.BlockSpec((cs, wk), lambda b, h, c: (row(b, c), col_k // wk + h)),
            pl.BlockSpec((cs, wv), lambda b, h, c: (row(b, c), col_v // wv + h)),
            pl.BlockSpec((cs, wv), lambda b, h, c: (row(b, c), col_g // wv + h)),
            pl.BlockSpec((cs, LANES), lambda b, h, c: (row(b, c), 0)),
            pl.BlockSpec((LANES, wk), lambda b, h, c: (0, h)),
            pl.BlockSpec((LANES, wk), lambda b, h, c: (0, h)),
            pl.BlockSpec((1, wk), lambda b, h, c: (0, h)),
            pl.BlockSpec((1, dv), lambda b, h, c: (0, 0)),
        ],
        out_specs=pl.BlockSpec((cs, wv), lambda b, h, c: (row(b, c), h)),
        out_shape=jax.ShapeDtypeStruct((batch * seq, heads * dv), BF16),
        scratch_shapes=[pltpu.VMEM((nh, dv, dk), F32)],
        compiler_params=_params(("parallel", "parallel", "arbitrary")),
        name="gla",
    )(proj, proj, proj, proj, ga, w2_hi, w2_lo, a_b.reshape(1, -1), hn_g.reshape(1, -1))


def _sbproj_kernel(a_ref, w_ref, q_ref, k_ref, v_ref, sg_ref, p_ref, o_ref,
                   wb_ref, carry_ref, acc_ref, *, tq, nq, nh, d, scale2):
    i = pl.program_id(1)
    qi = (pl.program_id(0) * pl.num_programs(1) + i) % nq

    @pl.when(i == 0)
    def _():
        for r in range(0, wb_ref.shape[0], CAST_ROWS):
            rows = slice(r, r + CAST_ROWS)
            wb_ref[rows, :] = w_ref[rows, :].astype(BF16)

    r = lax.broadcasted_iota(jnp.int32, (2 * LANES, 2 * LANES), 0) % LANES
    c = lax.broadcasted_iota(jnp.int32, (2 * LANES, 2 * LANES), 1)
    cum_w = jnp.where((c >= LANES) | (r > c), 1.0, 0.0).astype(BF16)
    groups = tq // LANES

    piece = p_ref.shape[1] // 4

    def project(n):
        cols = slice(n * piece, (n + 1) * piece)
        p_ref[:, cols] = _dot_nt(a_ref[...], wb_ref[cols, :]).astype(p_ref.dtype)

    def sweep(key_start, mask, first=False, pieces=None):
        heads = [slice(h * d, (h + 1) * d) for h in range(nh)]
        u = [_dot_nt(q_ref[:, cols], k_ref[pl.ds(key_start, tq), cols]) * scale2
             for cols in heads]
        if pieces:
            project(pieces[0])
        sp, after = [], []
        for h in range(nh):
            sp_h = jnp.maximum(
                jnp.log2(1.0 + jnp.exp2(jnp.minimum(u[h], SB_EXP2_CLAMP))), u[h])
            sp_m = sp_h if mask is None else jnp.where(mask, sp_h, 0.0)
            carry = jnp.zeros((tq, LANES), F32) if first else carry_ref[h]
            after_h = [None] * groups
            for g in reversed(range(groups)):
                hi, lo = _split_bf16(sp_m[:, g * LANES:(g + 1) * LANES])
                ct = _dot(jnp.concatenate([hi, lo], axis=1), cum_w)
                after_h[g] = ct[:, :LANES] + carry
                carry = carry + ct[:, LANES:]
            carry_ref[h] = carry
            sp.append(sp_h)
            after.append(jnp.concatenate(after_h, axis=1))
        if pieces:
            project(pieces[1])
        for h in range(nh):
            w = jnp.exp2(u[h] - sp[h] - after[h])
            if mask is not None:
                w = jnp.where(mask, w, 0.0)
            pv = _dot(w.astype(BF16), v_ref[pl.ds(key_start, tq), heads[h]])
            acc_ref[h] = pv if first else acc_ref[h] + pv

    def least_carry():
        m = carry_ref[0]
        for h in range(1, nh):
            m = jnp.minimum(m, carry_ref[h])
        return jnp.min(m)

    t_idx = lax.broadcasted_iota(jnp.int32, (tq, tq), 0)
    s_idx = lax.broadcasted_iota(jnp.int32, (tq, tq), 1)
    sweep(pl.multiple_of(qi * tq, tq), s_idx < t_idx, first=True, pieces=(0, 1))
    @pl.when(qi >= 1)
    def _():
        sweep(pl.multiple_of((qi - 1) * tq, tq), None, pieces=(2, 3))

    @pl.when(qi == 0)
    def _():
        project(2)
        project(3)

    def cond(state):
        blk, low = state
        return jnp.logical_and(blk >= 0, low < SB_LOG2_CUTOFF)

    def body(state):
        blk, _ = state
        sweep(pl.multiple_of(blk * tq, tq), None)
        return blk - 1, least_carry()

    lax.while_loop(cond, body, (qi - 2, least_carry()))

    for h in range(nh):
        cols = slice(h * d, (h + 1) * d)
        sg = sg_ref[:, cols].astype(F32)
        o_ref[:, cols] = (acc_ref[h] * (sg * _sigmoid(sg))).astype(o_ref.dtype)


def _sbproj(a, w_t, proj, *, row0, width, tm, tn, batch, seq, heads, d,
            col_q, col_k, col_v, col_g, tq, nh):
    m, k = a.shape
    nq = seq // tq
    hgroups = heads // nh
    w = nh * d
    n_j, n_i = width // tn, m // tm
    assert n_j * n_i == batch * hgroups * nq, "one stick-breaking step per projection tile"

    def sb_pos(j, i):
        s = j * n_i + i
        return s // (hgroups * nq), (s // nq) % hgroups, s % nq

    def qmap(col):
        def index(j, i):
            b, g, qi = sb_pos(j, i)
            return b * nq + qi, col // w + g
        return index

    def kvmap(col):
        def index(j, i):
            b, g, _ = sb_pos(j, i)
            return b, col // w + g
        return index

    kern = functools.partial(_sbproj_kernel, tq=tq, nq=nq, nh=nh, d=d,
                             scale2=math.log2(math.e) / math.sqrt(d))
    return pl.pallas_call(
        kern,
        grid=(n_j, n_i),
        in_specs=[
            pl.BlockSpec((tm, k), lambda j, i: (i, 0)),
            pl.BlockSpec((pl.Element(tn), pl.Element(k)),
                         lambda j, i: ((row0 // SUBLANES + j * (tn // SUBLANES)) * SUBLANES, 0)),
            pl.BlockSpec((tq, w), qmap(col_q)),
            pl.BlockSpec((seq, w), kvmap(col_k)),
            pl.BlockSpec((seq, w), kvmap(col_v)),
            pl.BlockSpec((tq, w), qmap(col_g)),
        ],
        out_specs=[
            pl.BlockSpec((tm, tn), lambda j, i: (i, j)),
            pl.BlockSpec((tq, w), qmap(0)),
        ],
        out_shape=[
            jax.ShapeDtypeStruct((m, width), BF16),
            jax.ShapeDtypeStruct((batch * seq, heads * d), BF16),
        ],
        scratch_shapes=[pltpu.VMEM((tn, k), BF16),
                        pltpu.VMEM((nh, tq, LANES), F32), pltpu.VMEM((nh, tq, d), F32)],
        compiler_params=_params(("arbitrary", "arbitrary")),
        name="sbproj",
    )(a, w_t, proj, proj, proj, proj)


def _memattn_kernel(q_ref, k_ref, v_ref, o_ref, *, nh, d, scale):
    heads = [slice(h * d, (h + 1) * d) for h in range(nh)]
    scores = [_dot_nt(q_ref[:, cols], k_ref[:, cols]) * scale for cols in heads]
    for cols, s in zip(heads, scores):
        s = s - jnp.max(s, axis=-1, keepdims=True)
        p = jnp.exp(s)
        p = p / jnp.sum(p, axis=-1, keepdims=True)
        o_ref[:, cols] = _dot(p.astype(BF16), v_ref[:, cols]).astype(o_ref.dtype)


def _memattn(proj, mkv, *, batch, seq, heads, d, col_q, tq, nh):
    nq = seq // tq
    w = nh * d
    kern = functools.partial(_memattn_kernel, nh=nh, d=d, scale=1.0 / math.sqrt(d))
    return pl.pallas_call(
        kern,
        grid=(batch, heads // nh, nq),
        in_specs=[
            pl.BlockSpec((tq, w), lambda b, h, i: (b * nq + i, col_q // w + h)),
            pl.BlockSpec((N_MEM, w), lambda b, h, i: (b, h)),
            pl.BlockSpec((N_MEM, w), lambda b, h, i: (b, heads // nh + h)),
        ],
        out_specs=pl.BlockSpec((tq, w), lambda b, h, i: (b * nq + i, h)),
        out_shape=jax.ShapeDtypeStruct((batch * seq, heads * d), BF16),
        compiler_params=_params(("parallel", "parallel", "parallel")),
        name="memattn",
    )(proj, mkv, mkv)


def _merge_kernel(oa_ref, ob_ref, om_ref, wa_ref, wb_ref, wm_ref,
                  ga_ref, gb_ref, gm_ref, o_ref):
    acc = _sigmoid(ga_ref[...].astype(F32)) * _dot(oa_ref[...], wa_ref[...])
    acc += _sigmoid(gb_ref[...].astype(F32)) * _dot(ob_ref[...], wb_ref[...])
    acc += _sigmoid(gm_ref[...].astype(F32)) * _dot(om_ref[...], wm_ref[...])
    o_ref[...] = acc.astype(o_ref.dtype)


def _merge(o_a, o_b, o_m, w_a, w_b, w_m, proj, *, col_gates, tm, tn):
    m, d = o_a.shape
    gate_blk = col_gates // tn
    act = lambda: pl.BlockSpec((tm, d), lambda i, j: (i, 0))
    wgt = lambda: pl.BlockSpec((d, tn), lambda i, j: (0, j))
    gate = lambda n: pl.BlockSpec((tm, tn), lambda i, j: (i, gate_blk + n * (d // tn) + j))
    return pl.pallas_call(
        _merge_kernel,
        grid=(m // tm, d // tn),
        in_specs=[act(), act(), act(), wgt(), wgt(), wgt(), gate(0), gate(1), gate(2)],
        out_specs=pl.BlockSpec((tm, tn), lambda i, j: (i, j)),
        out_shape=jax.ShapeDtypeStruct((m, d), BF16),
        compiler_params=_params(("parallel", "parallel")),
        name="merge",
    )(o_a, o_b, o_m, w_a, w_b, w_m, proj, proj, proj)


def _out_kernel(m_ref, w_ref, g_ref, x_ref, o_ref):
    y = _dot(m_ref[...], w_ref[...])
    ms = jnp.mean(y * y, axis=-1, keepdims=True)
    o_ref[...] = x_ref[...] + y * lax.rsqrt(ms + EPS) * g_ref[...]


def _out(merged, w_out, g, x2d, tm):
    m, d = x2d.shape
    return pl.pallas_call(
        _out_kernel,
        grid=(m // tm,),
        in_specs=[
            pl.BlockSpec((tm, d), lambda i: (i, 0)),
            pl.BlockSpec((d, d), lambda i: (0, 0)),
            pl.BlockSpec((1, d), lambda i: (0, 0)),
            pl.BlockSpec((tm, d), lambda i: (i, 0)),
        ],
        out_specs=pl.BlockSpec((tm, d), lambda i: (i, 0)),
        out_shape=jax.ShapeDtypeStruct((m, d), F32),
        compiler_params=_params(("parallel",)),
        name="outproj",
    )(merged, w_out, g.reshape(1, d), x2d)


def kernel(x, mem, norm_pre_g, norm_post_g, norm_mem_g, w_in, gla_a_w2, gla_a_b,
           gla_head_norm_g, w_mem_kv, w_proj_gla, w_proj_sb, w_proj_mem, w_out):
    batch, seq, d = x.shape
    dk_total, dv_total = d // 2, d
    dk, dv = dk_total // GLA_HEADS, dv_total // GLA_HEADS
    sb_heads = d // SB_HEAD_DIM
    mem_d = d // MEM_HEADS

    c_ga = 2 * dk_total + 2 * dv_total

    x2d = x.reshape(batch * seq, d)
    mem2d = mem.reshape(batch * N_MEM, d)

    mkv = _memkv(mem2d, norm_mem_g, w_mem_kv, tn=512)
    w_in_t = w_in.T
    h, ga = _prenorm(x2d, norm_pre_g, w_in_t, c_ga, tm=512)
    c_sb = c_ga + GLA_RANK
    proj_a = _inproj(h, w_in_t, row0=0, width=c_ga, tm=2048, tn=1024, name="inproj_a")
    proj_s = _inproj(h, w_in_t, row0=c_sb, width=4 * d, tm=2048, tn=1024, name="inproj_s")

    o_a = _gla(proj_a, ga, gla_a_w2, gla_a_b, gla_head_norm_g, batch=batch, seq=seq,
               heads=GLA_HEADS, dk=dk, dv=dv, col_q=0, col_k=dk_total, col_v=2 * dk_total,
               col_g=2 * dk_total + dv_total, cs=256, nh=4)
    proj_m, o_b = _sbproj(h, w_in_t, proj_s, row0=c_sb + 4 * d, width=(1 + N_BRANCHES) * d,
                          tm=512, tn=1024, batch=batch, seq=seq, heads=sb_heads, d=SB_HEAD_DIM,
                          col_q=0, col_k=d, col_v=2 * d, col_g=3 * d, tq=256, nh=4)
    o_m = _memattn(proj_m, mkv, batch=batch, seq=seq, heads=MEM_HEADS, d=mem_d,
                   col_q=0, tq=256, nh=4)

    merged = _merge(o_a, o_b, o_m, w_proj_gla.astype(BF16), w_proj_sb.astype(BF16),
                    w_proj_mem.astype(BF16), proj_m, col_gates=d, tm=1024, tn=512)
    out = _out(merged, w_out.astype(BF16), norm_post_g, x2d, tm=512)
    return out.reshape(batch, seq, d)
```

```python
import functools
import math

import jax
import jax.numpy as jnp
from jax import lax
from jax.experimental import pallas as pl
from jax.experimental.pallas import tpu as pltpu

F32 = jnp.float32
BF16 = jnp.bfloat16

EPS = 1e-6
N_MEM = 256
GLA_HEADS = 4
GLA_RANK = 16
GLA_TAU = 16.0
SB_HEAD_DIM = 128
MEM_HEADS = 4
N_BRANCHES = 3

LANES = 128
SUBLANES = 8
VMEM_LIMIT = 56 * 1024 * 1024
SB_LOG2_CUTOFF = 90.0 * math.log2(math.e)
SB_EXP2_CLAMP = 126.0


def _params(sem):
    return pltpu.CompilerParams(dimension_semantics=sem, vmem_limit_bytes=VMEM_LIMIT)


def _split_bf16(x):
    hi = x.astype(BF16)
    lo = (x - hi.astype(F32)).astype(BF16)
    return hi, lo


def _dot(a, b):
    return jnp.dot(a, b, preferred_element_type=F32)


def _dot_nt(a, b):
    return lax.dot_general(a, b, (((1,), (1,)), ((), ())), preferred_element_type=F32)


def _dot_tn(a, b):
    return lax.dot_general(a, b, (((0,), (0,)), ((), ())), preferred_element_type=F32)


def _softplus(z):
    return jnp.maximum(z, 0.0) + jnp.log(1.0 + jnp.exp(-jnp.abs(z)))


def _sigmoid(z):
    return 1.0 / (1.0 + jnp.exp(-z))


def _prenorm_kernel(x_ref, g_ref, wga_ref, h_ref, ga_ref):
    x = x_ref[...]
    ms = jnp.mean(x * x, axis=-1, keepdims=True)
    h = (x * lax.rsqrt(ms + EPS) * g_ref[...]).astype(BF16)
    h_ref[...] = h
    ga_ref[...] = _dot_nt(h, wga_ref[...].astype(BF16))


def _prenorm(x2d, g, w_in_t, col_ga, tm):
    m, d = x2d.shape
    return pl.pallas_call(
        _prenorm_kernel,
        grid=(m // tm,),
        in_specs=[
            pl.BlockSpec((tm, d), lambda i: (i, 0)),
            pl.BlockSpec((1, d), lambda i: (0, 0)),
            pl.BlockSpec((LANES, d), lambda i: (col_ga // LANES, 0)),
        ],
        out_specs=[
            pl.BlockSpec((tm, d), lambda i: (i, 0)),
            pl.BlockSpec((tm, LANES), lambda i: (i, 0)),
        ],
        out_shape=[
            jax.ShapeDtypeStruct((m, d), BF16),
            jax.ShapeDtypeStruct((m, LANES), F32),
        ],
        compiler_params=_params(("parallel",)),
        name="prenorm",
    )(x2d, g.reshape(1, d), w_in_t)


CAST_ROWS = 128


def _inproj_kernel(a_ref, w_ref, o_ref, wb_ref):
    @pl.when(pl.program_id(1) == 0)
    def _():
        for r in range(0, wb_ref.shape[0], CAST_ROWS):
            rows = slice(r, r + CAST_ROWS)
            wb_ref[rows, :] = w_ref[rows, :].astype(BF16)

    o_ref[...] = _dot_nt(a_ref[...], wb_ref[...]).astype(o_ref.dtype)


def _inproj(a, w_t, *, row0, width, tm, tn, name):
    m, k = a.shape
    return pl.pallas_call(
        _inproj_kernel,
        grid=(width // tn, m // tm),
        in_specs=[
            pl.BlockSpec((tm, k), lambda j, i: (i, 0)),
            pl.BlockSpec((pl.Element(tn), pl.Element(k)),
                         lambda j, i: ((row0 // SUBLANES + j * (tn // SUBLANES)) * SUBLANES, 0)),
        ],
        out_specs=pl.BlockSpec((tm, tn), lambda j, i: (i, j)),
        out_shape=jax.ShapeDtypeStruct((m, width), BF16),
        scratch_shapes=[pltpu.VMEM((tn, k), BF16)],
        compiler_params=_params(("parallel", "arbitrary")),
        name=name,
    )(a, w_t)


def _memkv_kernel(m_ref, g_ref, w_ref, o_ref):
    x = m_ref[...]
    ms = jnp.mean(x * x, axis=-1, keepdims=True)
    h = (x * lax.rsqrt(ms + EPS) * g_ref[...]).astype(BF16)
    o_ref[...] = _dot(h, w_ref[...].astype(BF16)).astype(o_ref.dtype)


def _memkv(mem2d, g, w, tn):
    m, d = mem2d.shape
    n = w.shape[1]
    return pl.pallas_call(
        _memkv_kernel,
        grid=(n // tn,),
        in_specs=[
            pl.BlockSpec((m, d), lambda j: (0, 0)),
            pl.BlockSpec((1, d), lambda j: (0, 0)),
            pl.BlockSpec((d, tn), lambda j: (0, j)),
        ],
        out_specs=pl.BlockSpec((m, tn), lambda j: (0, j)),
        out_shape=jax.ShapeDtypeStruct((m, n), BF16),
        compiler_params=_params(("parallel",)),
        name="memkv",
    )(mem2d, g.reshape(1, d), w)


def _gla_kernel(q_ref, k_ref, v_ref, gg_ref, ga_ref, w2h_ref, w2l_ref, ab_ref, hn_ref,
                o_ref, st_ref, *, cs, nh, dk, dv, scale):
    @pl.when(pl.program_id(2) == 0)
    def _():
        st_ref[...] = jnp.zeros_like(st_ref)

    ga_hi, ga_lo = _split_bf16(ga_ref[...])
    w2_hi = w2h_ref[...]
    z = _dot(ga_hi, w2_hi) + _dot(ga_lo, w2_hi) + _dot(ga_hi, w2l_ref[...]) + ab_ref[...]
    log_a = -_softplus(-z) * (math.log2(math.e) / GLA_TAU)

    row = lax.broadcasted_iota(jnp.int32, (cs, cs), 0)
    col = lax.broadcasted_iota(jnp.int32, (cs, cs), 1)
    causal = col <= row
    tri = jnp.where(causal, 1.0, 0.0).astype(BF16)
    la_hi, la_lo = _split_bf16(log_a)
    b_all = _dot(tri, la_hi) + _dot(tri, la_lo)

    kcols = [slice(h * dk, (h + 1) * dk) for h in range(nh)]
    vcols = [slice(h * dv, (h + 1) * dv) for h in range(nh)]
    for h in range(nh):
        b = b_all[:, kcols[h]]
        b_last = b[cs - 1:cs, :]
        b_mid = b[cs // 2 - 1:cs // 2, :]
        q = q_ref[:, kcols[h]].astype(F32) * scale
        k = k_ref[:, kcols[h]].astype(F32)
        v = v_ref[:, vcols[h]]
        q_inter = (q * jnp.exp2(b)).astype(BF16)
        k_carry = (k * jnp.exp2(b_last - b)).astype(BF16)
        q_intra = (q * jnp.exp2(b - b_mid)).astype(BF16)
        k_intra = (k * jnp.exp2(b_mid - b)).astype(BF16)

        attn = jnp.where(causal, _dot_nt(q_intra, k_intra), 0.0).astype(BF16)
        st = st_ref[h]
        o = _dot_nt(q_inter, st.astype(BF16)) + _dot(attn, v)
        st_ref[h] = st * jnp.exp2(b_last) + _dot_tn(v, k_carry)

        ms = jnp.mean(o * o, axis=-1, keepdims=True)
        o = o * lax.rsqrt(ms + EPS) * hn_ref[...]
        gg = gg_ref[:, vcols[h]].astype(F32)
        o_ref[:, vcols[h]] = (o * (gg * _sigmoid(gg))).astype(o_ref.dtype)


def _gla(proj, ga, w2, a_b, hn_g, *, batch, seq, heads, dk, dv, col_q, col_k,
         col_v, col_g, cs, nh):
    nc = seq // cs
    wk, wv = nh * dk, nh * dv
    w2_hi, w2_lo = _split_bf16(jnp.pad(w2, ((0, LANES - w2.shape[0]), (0, 0))))
    kern = functools.partial(_gla_kernel, cs=cs, nh=nh, dk=dk, dv=dv, scale=dk ** -0.5)
    row = lambda b, c: b * nc + c
    return pl.pallas_call(
        kern,
        grid=(batch, heads // nh, nc),
        in_specs=[
            pl.BlockSpec((cs, wk), lambda b, h, c: (row(b, c), col_q // wk + h)),
            pl.BlockSpec((cs, wk), lambda b, h, c: (row(b, c), col_k // wk + h)),
            pl.BlockSpec((cs, wv), lambda b, h, c: (row(b, c), col_v // wv + h)),
            pl.BlockSpec((cs, wv), lambda b, h, c: (row(b, c), col_g // wv + h)),
            pl.BlockSpec((cs, LANES), lambda b, h, c: (row(b, c), 0)),
            pl.BlockSpec((LANES, wk), lambda b, h, c: (0, h)),
            pl.BlockSpec((LANES, wk), lambda b, h, c: (0, h)),
            pl.BlockSpec((1, wk), lambda b, h, c: (0, h)),
            pl.BlockSpec((1, dv), lambda b, h, c: (0, 0)),
        ],
        out_specs=pl.BlockSpec((cs, wv), lambda b, h, c: (row(b, c), h)),
        out_shape=jax.ShapeDtypeStruct((batch * seq, heads * dv), BF16),
        scratch_shapes=[pltpu.VMEM((nh, dv, dk), F32)],
        compiler_params=_params(("parallel", "parallel", "arbitrary")),
        name="gla",
    )(proj, proj, proj, proj, ga, w2_hi, w2_lo, a_b.reshape(1, -1), hn_g.reshape(1, -1))


def _sb_kernel(q_ref, k_ref, v_ref, sg_ref, o_ref, carry_ref, acc_ref, *, tq, nh, d, scale2):
    qi = pl.program_id(2)

    r = lax.broadcasted_iota(jnp.int32, (2 * LANES, 2 * LANES), 0) % LANES
    c = lax.broadcasted_iota(jnp.int32, (2 * LANES, 2 * LANES), 1)
    cum_w = jnp.where((c >= LANES) | (r > c), 1.0, 0.0).astype(BF16)
    groups = tq // LANES

    def sweep(key_start, mask, first=False):
        heads = [slice(h * d, (h + 1) * d) for h in range(nh)]
        u = [_dot_nt(q_ref[:, cols], k_ref[pl.ds(key_start, tq), cols]) * scale2
             for cols in heads]
        sp, after = [], []
        for h in range(nh):
            sp_h = jnp.maximum(
                jnp.log2(1.0 + jnp.exp2(jnp.minimum(u[h], SB_EXP2_CLAMP))), u[h])
            sp_m = sp_h if mask is None else jnp.where(mask, sp_h, 0.0)
            carry = jnp.zeros((tq, LANES), F32) if first else carry_ref[h]
            after_h = [None] * groups
            for g in reversed(range(groups)):
                hi, lo = _split_bf16(sp_m[:, g * LANES:(g + 1) * LANES])
                ct = _dot(jnp.concatenate([hi, lo], axis=1), cum_w)
                after_h[g] = ct[:, :LANES] + carry
                carry = carry + ct[:, LANES:]
            carry_ref[h] = carry
            sp.append(sp_h)
            after.append(jnp.concatenate(after_h, axis=1))
        for h in range(nh):
            w = jnp.exp2(u[h] - sp[h] - after[h])
            if mask is not None:
                w = jnp.where(mask, w, 0.0)
            pv = _dot(w.astype(BF16), v_ref[pl.ds(key_start, tq), heads[h]])
            acc_ref[h] = pv if first else acc_ref[h] + pv

    def least_carry():
        m = carry_ref[0]
        for h in range(1, nh):
            m = jnp.minimum(m, carry_ref[h])
        return jnp.min(m)

    t_idx = lax.broadcasted_iota(jnp.int32, (tq, tq), 0)
    s_idx = lax.broadcasted_iota(jnp.int32, (tq, tq), 1)
    sweep(pl.multiple_of(qi * tq, tq), s_idx < t_idx, first=True)

    def cond(state):
        blk, low = state
        return jnp.logical_and(blk >= 0, low < SB_LOG2_CUTOFF)

    def body(state):
        blk, _ = state
        sweep(pl.multiple_of(blk * tq, tq), None)
        return blk - 1, least_carry()

    lax.while_loop(cond, body, (qi - 1, least_carry()))

    for h in range(nh):
        cols = slice(h * d, (h + 1) * d)
        sg = sg_ref[:, cols].astype(F32)
        o_ref[:, cols] = (acc_ref[h] * (sg * _sigmoid(sg))).astype(o_ref.dtype)


def _sb(proj, *, batch, seq, heads, d, col_q, col_k, col_v, col_g, tq, nh):
    nq = seq // tq
    w = nh * d
    kern = functools.partial(_sb_kernel, tq=tq, nh=nh, d=d, scale2=math.log2(math.e) / math.sqrt(d))
    return pl.pallas_call(
        kern,
        grid=(batch, heads // nh, nq),
        in_specs=[
            pl.BlockSpec((tq, w), lambda b, h, i: (b * nq + i, col_q // w + h)),
            pl.BlockSpec((seq, w), lambda b, h, i: (b, col_k // w + h)),
            pl.BlockSpec((seq, w), lambda b, h, i: (b, col_v // w + h)),
            pl.BlockSpec((tq, w), lambda b, h, i: (b * nq + i, col_g // w + h)),
        ],
        out_specs=pl.BlockSpec((tq, w), lambda b, h, i: (b * nq + i, h)),
        out_shape=jax.ShapeDtypeStruct((batch * seq, heads * d), BF16),
        scratch_shapes=[pltpu.VMEM((nh, tq, LANES), F32), pltpu.VMEM((nh, tq, d), F32)],
        compiler_params=_params(("parallel", "parallel", "arbitrary")),
        name="stickbreak",
    )(proj, proj, proj, proj)


def _memattn_kernel(q_ref, k_ref, v_ref, o_ref, *, nh, d, scale):
    heads = [slice(h * d, (h + 1) * d) for h in range(nh)]
    scores = [_dot_nt(q_ref[:, cols], k_ref[:, cols]) * scale for cols in heads]
    for cols, s in zip(heads, scores):
        s = s - jnp.max(s, axis=-1, keepdims=True)
        p = jnp.exp(s)
        p = p / jnp.sum(p, axis=-1, keepdims=True)
        o_ref[:, cols] = _dot(p.astype(BF16), v_ref[:, cols]).astype(o_ref.dtype)


def _memattn(proj, mkv, *, batch, seq, heads, d, col_q, tq, nh):
    nq = seq // tq
    w = nh * d
    kern = functools.partial(_memattn_kernel, nh=nh, d=d, scale=1.0 / math.sqrt(d))
    return pl.pallas_call(
        kern,
        grid=(batch, heads // nh, nq),
        in_specs=[
            pl.BlockSpec((tq, w), lambda b, h, i: (b * nq + i, col_q // w + h)),
            pl.BlockSpec((N_MEM, w), lambda b, h, i: (b, h)),
            pl.BlockSpec((N_MEM, w), lambda b, h, i: (b, heads // nh + h)),
        ],
        out_specs=pl.BlockSpec((tq, w), lambda b, h, i: (b * nq + i, h)),
        out_shape=jax.ShapeDtypeStruct((batch * seq, heads * d), BF16),
        compiler_params=_params(("parallel", "parallel", "parallel")),
        name="memattn",
    )(proj, mkv, mkv)


def _merge_kernel(oa_ref, ob_ref, om_ref, wa_ref, wb_ref, wm_ref,
                  ga_ref, gb_ref, gm_ref, o_ref):
    acc = _sigmoid(ga_ref[...].astype(F32)) * _dot(oa_ref[...], wa_ref[...])
    acc += _sigmoid(gb_ref[...].astype(F32)) * _dot(ob_ref[...], wb_ref[...])
    acc += _sigmoid(gm_ref[...].astype(F32)) * _dot(om_ref[...], wm_ref[...])
    o_ref[...] = acc.astype(o_ref.dtype)


def _merge(o_a, o_b, o_m, w_a, w_b, w_m, proj, *, col_gates, tm, tn):
    m, d = o_a.shape
    gate_blk = col_gates // tn
    act = lambda: pl.BlockSpec((tm, d), lambda i, j: (i, 0))
    wgt = lambda: pl.BlockSpec((d, tn), lambda i, j: (0, j))
    gate = lambda n: pl.BlockSpec((tm, tn), lambda i, j: (i, gate_blk + n * (d // tn) + j))
    return pl.pallas_call(
        _merge_kernel,
        grid=(m // tm, d // tn),
        in_specs=[act(), act(), act(), wgt(), wgt(), wgt(), gate(0), gate(1), gate(2)],
        out_specs=pl.BlockSpec((tm, tn), lambda i, j: (i, j)),
        out_shape=jax.ShapeDtypeStruct((m, d), BF16),
        compiler_params=_params(("parallel", "parallel")),
        name="merge",
    )(o_a, o_b, o_m, w_a, w_b, w_m, proj, proj, proj)


def _out_kernel(m_ref, w_ref, g_ref, x_ref, o_ref):
    y = _dot(m_ref[...], w_ref[...])
    ms = jnp.mean(y * y, axis=-1, keepdims=True)
    o_ref[...] = x_ref[...] + y * lax.rsqrt(ms + EPS) * g_ref[...]


def _out(merged, w_out, g, x2d, tm):
    m, d = x2d.shape
    return pl.pallas_call(
        _out_kernel,
        grid=(m // tm,),
        in_specs=[
            pl.BlockSpec((tm, d), lambda i: (i, 0)),
            pl.BlockSpec((d, d), lambda i: (0, 0)),
            pl.BlockSpec((1, d), lambda i: (0, 0)),
            pl.BlockSpec((tm, d), lambda i: (i, 0)),
        ],
        out_specs=pl.BlockSpec((tm, d), lambda i: (i, 0)),
        out_shape=jax.ShapeDtypeStruct((m, d), F32),
        compiler_params=_params(("parallel",)),
        name="outproj",
    )(merged, w_out, g.reshape(1, d), x2d)


def kernel(x, mem, norm_pre_g, norm_post_g, norm_mem_g, w_in, gla_a_w2, gla_a_b,
           gla_head_norm_g, w_mem_kv, w_proj_gla, w_proj_sb, w_proj_mem, w_out):
    batch, seq, d = x.shape
    dk_total, dv_total = d // 2, d
    dk, dv = dk_total // GLA_HEADS, dv_total // GLA_HEADS
    sb_heads = d // SB_HEAD_DIM
    mem_d = d // MEM_HEADS

    c_ga = 2 * dk_total + 2 * dv_total

    x2d = x.reshape(batch * seq, d)
    mem2d = mem.reshape(batch * N_MEM, d)

    mkv = _memkv(mem2d, norm_mem_g, w_mem_kv, tn=512)
    w_in_t = w_in.T
    h, ga = _prenorm(x2d, norm_pre_g, w_in_t, c_ga, tm=512)
    c_sb = c_ga + GLA_RANK
    proj_a = _inproj(h, w_in_t, row0=0, width=c_ga, tm=2048, tn=1024, name="inproj_a")
    proj_s = _inproj(h, w_in_t, row0=c_sb, width=4 * d, tm=2048, tn=1024, name="inproj_s")

    o_a = _gla(proj_a, ga, gla_a_w2, gla_a_b, gla_head_norm_g, batch=batch, seq=seq,
               heads=GLA_HEADS, dk=dk, dv=dv, col_q=0, col_k=dk_total, col_v=2 * dk_total,
               col_g=2 * dk_total + dv_total, cs=256, nh=4)
    proj_m = _inproj(h, w_in_t, row0=c_sb + 4 * d, width=(1 + N_BRANCHES) * d,
                     tm=2048, tn=1024, name="inproj_m")
    o_b = _sb(proj_s, batch=batch, seq=seq, heads=sb_heads, d=SB_HEAD_DIM,
              col_q=0, col_k=d, col_v=2 * d, col_g=3 * d, tq=256, nh=8)
    o_m = _memattn(proj_m, mkv, batch=batch, seq=seq, heads=MEM_HEADS, d=mem_d,
                   col_q=0, tq=256, nh=4)

    merged = _merge(o_a, o_b, o_m, w_proj_gla.astype(BF16), w_proj_sb.astype(BF16),
                    w_proj_mem.astype(BF16), proj_m, col_gates=d, tm=1024, tn=512)
    out = _out(merged, w_out.astype(BF16), norm_post_g, x2d, tm=512)
    return out.reshape(batch, seq, d)
```

```python
import functools
import math

import jax
import jax.numpy as jnp
from jax import lax
from jax.experimental import pallas as pl
from jax.experimental.pallas import tpu as pltpu

F32 = jnp.float32
BF16 = jnp.bfloat16

EPS = 1e-6
N_MEM = 256
GLA_HEADS = 4
GLA_RANK = 16
GLA_TAU = 16.0
SB_HEAD_DIM = 128
MEM_HEADS = 4
N_BRANCHES = 3

LANES = 128
SUBLANES = 8
VMEM_LIMIT = 56 * 1024 * 1024
SB_LOG2_CUTOFF = 90.0 * math.log2(math.e)
SB_EXP2_CLAMP = 126.0


def _params(sem):
    return pltpu.CompilerParams(dimension_semantics=sem, vmem_limit_bytes=VMEM_LIMIT)


def _split_bf16(x):
    hi = x.astype(BF16)
    lo = (x - hi.astype(F32)).astype(BF16)
    return hi, lo


def _dot(a, b):
    return jnp.dot(a, b, preferred_element_type=F32)


def _dot_nt(a, b):
    return lax.dot_general(a, b, (((1,), (1,)), ((), ())), preferred_element_type=F32)


def _dot_tn(a, b):
    return lax.dot_general(a, b, (((0,), (0,)), ((), ())), preferred_element_type=F32)


def _softplus(z):
    return jnp.maximum(z, 0.0) + jnp.log(1.0 + jnp.exp(-jnp.abs(z)))


def _sigmoid(z):
    return 1.0 / (1.0 + jnp.exp(-z))


def _prenorm_kernel(x_ref, g_ref, wga_ref, h_ref, ga_ref):
    x = x_ref[...]
    ms = jnp.mean(x * x, axis=-1, keepdims=True)
    h = (x * lax.rsqrt(ms + EPS) * g_ref[...]).astype(BF16)
    h_ref[...] = h
    ga_ref[...] = _dot_nt(h, wga_ref[...].astype(BF16))


def _prenorm(x2d, g, w_in_t, col_ga, tm):
    m, d = x2d.shape
    return pl.pallas_call(
        _prenorm_kernel,
        grid=(m // tm,),
        in_specs=[
            pl.BlockSpec((tm, d), lambda i: (i, 0)),
            pl.BlockSpec((1, d), lambda i: (0, 0)),
            pl.BlockSpec((LANES, d), lambda i: (col_ga // LANES, 0)),
        ],
        out_specs=[
            pl.BlockSpec((tm, d), lambda i: (i, 0)),
            pl.BlockSpec((tm, LANES), lambda i: (i, 0)),
        ],
        out_shape=[
            jax.ShapeDtypeStruct((m, d), BF16),
            jax.ShapeDtypeStruct((m, LANES), F32),
        ],
        compiler_params=_params(("parallel",)),
        name="prenorm",
    )(x2d, g.reshape(1, d), w_in_t)


CAST_ROWS = 128


def _inproj_kernel(a_ref, w_ref, o_ref, wb_ref):
    @pl.when(pl.program_id(1) == 0)
    def _():
        for r in range(0, wb_ref.shape[0], CAST_ROWS):
            rows = slice(r, r + CAST_ROWS)
            wb_ref[rows, :] = w_ref[rows, :].astype(BF16)

    o_ref[...] = _dot_nt(a_ref[...], wb_ref[...]).astype(o_ref.dtype)


def _inproj(a, w_t, *, row0, width, tm, tn, name):
    m, k = a.shape
    return pl.pallas_call(
        _inproj_kernel,
        grid=(width // tn, m // tm),
        in_specs=[
            pl.BlockSpec((tm, k), lambda j, i: (i, 0)),
            pl.BlockSpec((pl.Element(tn), pl.Element(k)),
                         lambda j, i: ((row0 // SUBLANES + j * (tn // SUBLANES)) * SUBLANES, 0)),
        ],
        out_specs=pl.BlockSpec((tm, tn), lambda j, i: (i, j)),
        out_shape=jax.ShapeDtypeStruct((m, width), BF16),
        scratch_shapes=[pltpu.VMEM((tn, k), BF16)],
        compiler_params=_params(("parallel", "arbitrary")),
        name=name,
    )(a, w_t)


def _memkv_kernel(m_ref, g_ref, w_ref, o_ref):
    x = m_ref[...]
    ms = jnp.mean(x * x, axis=-1, keepdims=True)
    h = (x * lax.rsqrt(ms + EPS) * g_ref[...]).astype(BF16)
    o_ref[...] = _dot(h, w_ref[...].astype(BF16)).astype(o_ref.dtype)


def _memkv(mem2d, g, w, tn):
    m, d = mem2d.shape
    n = w.shape[1]
    return pl.pallas_call(
        _memkv_kernel,
        grid=(n // tn,),
        in_specs=[
            pl.BlockSpec((m, d), lambda j: (0, 0)),
            pl.BlockSpec((1, d), lambda j: (0, 0)),
            pl.BlockSpec((d, tn), lambda j: (0, j)),
        ],
        out_specs=pl.BlockSpec((m, tn), lambda j: (0, j)),
        out_shape=jax.ShapeDtypeStruct((m, n), BF16),
        compiler_params=_params(("parallel",)),
        name="memkv",
    )(mem2d, g.reshape(1, d), w)


def _gla_kernel(q_ref, k_ref, v_ref, gg_ref, ga_ref, w2h_ref, w2l_ref, ab_ref, hn_ref,
                o_ref, st_ref, *, cs, nh, dk, dv, scale):
    @pl.when(pl.program_id(2) == 0)
    def _():
        st_ref[...] = jnp.zeros_like(st_ref)

    ga_hi, ga_lo = _split_bf16(ga_ref[...])
    w2_hi = w2h_ref[...]
    z = _dot(ga_hi, w2_hi) + _dot(ga_lo, w2_hi) + _dot(ga_hi, w2l_ref[...]) + ab_ref[...]
    log_a = -_softplus(-z) * (math.log2(math.e) / GLA_TAU)

    row = lax.broadcasted_iota(jnp.int32, (cs, cs), 0)
    col = lax.broadcasted_iota(jnp.int32, (cs, cs), 1)
    causal = col <= row
    tri = jnp.where(causal, 1.0, 0.0).astype(BF16)
    la_hi, la_lo = _split_bf16(log_a)
    b_all = _dot(tri, la_hi) + _dot(tri, la_lo)

    kcols = [slice(h * dk, (h + 1) * dk) for h in range(nh)]
    vcols = [slice(h * dv, (h + 1) * dv) for h in range(nh)]
    for h in range(nh):
        b = b_all[:, kcols[h]]
        b_last = b[cs - 1:cs, :]
        b_mid = b[cs // 2 - 1:cs // 2, :]
        q = q_ref[:, kcols[h]].astype(F32) * scale
        k = k_ref[:, kcols[h]].astype(F32)
        v = v_ref[:, vcols[h]]
        q_inter = (q * jnp.exp2(b)).astype(BF16)
        k_carry = (k * jnp.exp2(b_last - b)).astype(BF16)
        q_intra = (q * jnp.exp2(b - b_mid)).astype(BF16)
        k_intra = (k * jnp.exp2(b_mid - b)).astype(BF16)

        attn = jnp.where(causal, _dot_nt(q_intra, k_intra), 0.0).astype(BF16)
        st = st_ref[h]
        o = _dot_nt(q_inter, st.astype(BF16)) + _dot(attn, v)
        st_ref[h] = st * jnp.exp2(b_last) + _dot_tn(v, k_carry)

        ms = jnp.mean(o * o, axis=-1, keepdims=True)
        o = o * lax.rsqrt(ms + EPS) * hn_ref[...]
        gg = gg_ref[:, vcols[h]].astype(F32)
        o_ref[:, vcols[h]] = (o * (gg * _sigmoid(gg))).astype(o_ref.dtype)


def _gla(proj, ga, w2, a_b, hn_g, *, batch, seq, heads, dk, dv, col_q, col_k,
         col_v, col_g, cs, nh):
    nc = seq // cs
    wk, wv = nh * dk, nh * dv
    w2_hi, w2_lo = _split_bf16(jnp.pad(w2, ((0, LANES - w2.shape[0]), (0, 0))))
    kern = functools.partial(_gla_kernel, cs=cs, nh=nh, dk=dk, dv=dv, scale=dk ** -0.5)
    row = lambda b, c: b * nc + c
    return pl.pallas_call(
        kern,
        grid=(batch, heads // nh, nc),
        in_specs=[
            pl.BlockSpec((cs, wk), lambda b, h, c: (row(b, c), col_q // wk + h)),
            pl.BlockSpec((cs, wk), lambda b, h, c: (row(b, c), col_k // wk + h)),
            pl.BlockSpec((cs, wv), lambda b, h, c: (row(b, c), col_v // wv + h)),
            pl.BlockSpec((cs, wv), lambda b, h, c: (row(b, c), col_g // wv + h)),
            pl.BlockSpec((cs, LANES), lambda b, h, c: (row(b, c), 0)),
            pl.BlockSpec((LANES, wk), lambda b, h, c: (0, h)),
            pl.BlockSpec((LANES, wk), lambda b, h, c: (0, h)),
            pl.BlockSpec((1, wk), lambda b, h, c: (0, h)),
            pl.BlockSpec((1, dv), lambda b, h, c: (0, 0)),
        ],
        out_specs=pl.BlockSpec((cs, wv), lambda b, h, c: (row(b, c), h)),
        out_shape=jax.ShapeDtypeStruct((batch * seq, heads * dv), BF16),
        scratch_shapes=[pltpu.VMEM((nh, dv, dk), F32)],
        compiler_params=_params(("parallel", "parallel", "arbitrary")),
        name="gla",
    )(proj, proj, proj, proj, ga, w2_hi, w2_lo, a_b.reshape(1, -1), hn_g.reshape(1, -1))


def _sb_kernel(q_ref, k_ref, v_ref, sg_ref, o_ref, carry_ref, acc_ref, *, tq, nh, d, scale2):
    qi = pl.program_id(2)

    r = lax.broadcasted_iota(jnp.int32, (2 * LANES, 2 * LANES), 0) % LANES
    c = lax.broadcasted_iota(jnp.int32, (2 * LANES, 2 * LANES), 1)
    cum_w = jnp.where((c >= LANES) | (r > c), 1.0, 0.0).astype(BF16)
    groups = tq // LANES

    def sweep(key_start, mask, first=False):
        heads = [slice(h * d, (h + 1) * d) for h in range(nh)]
        u = [_dot_nt(q_ref[:, cols], k_ref[pl.ds(key_start, tq), cols]) * scale2
             for cols in heads]
        sp, after = [], []
        for h in range(nh):
            sp_h = jnp.maximum(
                jnp.log2(1.0 + jnp.exp2(jnp.minimum(u[h], SB_EXP2_CLAMP))), u[h])
            sp_m = sp_h if mask is None else jnp.where(mask, sp_h, 0.0)
            carry = jnp.zeros((tq, LANES), F32) if first else carry_ref[h]
            after_h = [None] * groups
            for g in reversed(range(groups)):
                hi, lo = _split_bf16(sp_m[:, g * LANES:(g + 1) * LANES])
                ct = _dot(jnp.concatenate([hi, lo], axis=1), cum_w)
                after_h[g] = ct[:, :LANES] + carry
                carry = carry + ct[:, LANES:]
            carry_ref[h] = carry
            sp.append(sp_h)
            after.append(jnp.concatenate(after_h, axis=1))
        for h in range(nh):
            w = jnp.exp2(u[h] - sp[h] - after[h])
            if mask is not None:
                w = jnp.where(mask, w, 0.0)
            pv = _dot(w.astype(BF16), v_ref[pl.ds(key_start, tq), heads[h]])
            acc_ref[h] = pv if first else acc_ref[h] + pv

    def least_carry():
        m = carry_ref[0]
        for h in range(1, nh):
            m = jnp.minimum(m, carry_ref[h])
        return jnp.min(m)

    t_idx = lax.broadcasted_iota(jnp.int32, (tq, tq), 0)
    s_idx = lax.broadcasted_iota(jnp.int32, (tq, tq), 1)
    sweep(pl.multiple_of(qi * tq, tq), s_idx < t_idx, first=True)

    def cond(state):
        blk, low = state
        return jnp.logical_and(blk >= 0, low < SB_LOG2_CUTOFF)

    def body(state):
        blk, _ = state
        sweep(pl.multiple_of(blk * tq, tq), None)
        return blk - 1, least_carry()

    lax.while_loop(cond, body, (qi - 1, least_carry()))

    for h in range(nh):
        cols = slice(h * d, (h + 1) * d)
        sg = sg_ref[:, cols].astype(F32)
        o_ref[:, cols] = (acc_ref[h] * (sg * _sigmoid(sg))).astype(o_ref.dtype)


def _sb(proj, *, batch, seq, heads, d, col_q, col_k, col_v, col_g, tq, nh):
    nq = seq // tq
    w = nh * d
    kern = functools.partial(_sb_kernel, tq=tq, nh=nh, d=d, scale2=math.log2(math.e) / math.sqrt(d))
    return pl.pallas_call(
        kern,
        grid=(batch, heads // nh, nq),
        in_specs=[
            pl.BlockSpec((tq, w), lambda b, h, i: (b * nq + i, col_q // w + h)),
            pl.BlockSpec((seq, w), lambda b, h, i: (b, col_k // w + h)),
            pl.BlockSpec((seq, w), lambda b, h, i: (b, col_v // w + h)),
            pl.BlockSpec((tq, w), lambda b, h, i: (b * nq + i, col_g // w + h)),
        ],
        out_specs=pl.BlockSpec((tq, w), lambda b, h, i: (b * nq + i, h)),
        out_shape=jax.ShapeDtypeStruct((batch * seq, heads * d), BF16),
        scratch_shapes=[pltpu.VMEM((nh, tq, LANES), F32), pltpu.VMEM((nh, tq, d), F32)],
        compiler_params=_params(("parallel", "parallel", "arbitrary")),
        name="stickbreak",
    )(proj, proj, proj, proj)


def _memattn_kernel(q_ref, k_ref, v_ref, o_ref, *, nh, d, scale):
    heads = [slice(h * d, (h + 1) * d) for h in range(nh)]
    scores = [_dot_nt(q_ref[:, cols], k_ref[:, cols]) * scale for cols in heads]
    for cols, s in zip(heads, scores):
        s = s - jnp.max(s, axis=-1, keepdims=True)
        p = jnp.exp(s)
        p = p / jnp.sum(p, axis=-1, keepdims=True)
        o_ref[:, cols] = _dot(p.astype(BF16), v_ref[:, cols]).astype(o_ref.dtype)


def _memattn(proj, mkv, *, batch, seq, heads, d, col_q, tq, nh):
    nq = seq // tq
    w = nh * d
    kern = functools.partial(_memattn_kernel, nh=nh, d=d, scale=1.0 / math.sqrt(d))
    return pl.pallas_call(
        kern,
        grid=(batch, heads // nh, nq),
        in_specs=[
            pl.BlockSpec((tq, w), lambda b, h, i: (b * nq + i, col_q // w + h)),
            pl.BlockSpec((N_MEM, w), lambda b, h, i: (b, h)),
            pl.BlockSpec((N_MEM, w), lambda b, h, i: (b, heads // nh + h)),
        ],
        out_specs=pl.BlockSpec((tq, w), lambda b, h, i: (b * nq + i, h)),
        out_shape=jax.ShapeDtypeStruct((batch * seq, heads * d), BF16),
        compiler_params=_params(("parallel", "parallel", "parallel")),
        name="memattn",
    )(proj, mkv, mkv)


def _merge_kernel(oa_ref, ob_ref, om_ref, wa_ref, wb_ref, wm_ref,
                  ga_ref, gb_ref, gm_ref, o_ref):
    acc = _sigmoid(ga_ref[...].astype(F32)) * _dot(oa_ref[...], wa_ref[...])
    acc += _sigmoid(gb_ref[...].astype(F32)) * _dot(ob_ref[...], wb_ref[...])
    acc += _sigmoid(gm_ref[...].astype(F32)) * _dot(om_ref[...], wm_ref[...])
    o_ref[...] = acc.astype(o_ref.dtype)


def _merge(o_a, o_b, o_m, w_a, w_b, w_m, proj, *, col_gates, tm, tn):
    m, d = o_a.shape
    gate_blk = col_gates // tn
    act = lambda: pl.BlockSpec((tm, d), lambda i, j: (i, 0))
    wgt = lambda: pl.BlockSpec((d, tn), lambda i, j: (0, j))
    gate = lambda n: pl.BlockSpec((tm, tn), lambda i, j: (i, gate_blk + n * (d // tn) + j))
    return pl.pallas_call(
        _merge_kernel,
        grid=(m // tm, d // tn),
        in_specs=[act(), act(), act(), wgt(), wgt(), wgt(), gate(0), gate(1), gate(2)],
        out_specs=pl.BlockSpec((tm, tn), lambda i, j: (i, j)),
        out_shape=jax.ShapeDtypeStruct((m, d), BF16),
        compiler_params=_params(("parallel", "parallel")),
        name="merge",
    )(o_a, o_b, o_m, w_a, w_b, w_m, proj, proj, proj)


def _out_kernel(m_ref, w_ref, g_ref, x_ref, o_ref):
    y = _dot(m_ref[...], w_ref[...])
    ms = jnp.mean(y * y, axis=-1, keepdims=True)
    o_ref[...] = x_ref[...] + y * lax.rsqrt(ms + EPS) * g_ref[...]


def _out(merged, w_out, g, x2d, tm):
    m, d = x2d.shape
    return pl.pallas_call(
        _out_kernel,
        grid=(m // tm,),
        in_specs=[
            pl.BlockSpec((tm, d), lambda i: (i, 0)),
            pl.BlockSpec((d, d), lambda i: (0, 0)),
            pl.BlockSpec((1, d), lambda i: (0, 0)),
            pl.BlockSpec((tm, d), lambda i: (i, 0)),
        ],
        out_specs=pl.BlockSpec((tm, d), lambda i: (i, 0)),
        out_shape=jax.ShapeDtypeStruct((m, d), F32),
        compiler_params=_params(("parallel",)),
        name="outproj",
    )(merged, w_out, g.reshape(1, d), x2d)


def kernel(x, mem, norm_pre_g, norm_post_g, norm_mem_g, w_in, gla_a_w2, gla_a_b,
           gla_head_norm_g, w_mem_kv, w_proj_gla, w_proj_sb, w_proj_mem, w_out):
    batch, seq, d = x.shape
    dk_total, dv_total = d // 2, d
    dk, dv = dk_total // GLA_HEADS, dv_total // GLA_HEADS
    sb_heads = d // SB_HEAD_DIM
    mem_d = d // MEM_HEADS

    c_ga = 2 * dk_total + 2 * dv_total

    x2d = x.reshape(batch * seq, d)
    mem2d = mem.reshape(batch * N_MEM, d)

    mkv = _memkv(mem2d, norm_mem_g, w_mem_kv, tn=1024)
    w_in_t = w_in.T
    h, ga = _prenorm(x2d, norm_pre_g, w_in_t, c_ga, tm=1024)
    c_sb = c_ga + GLA_RANK
    proj_a = _inproj(h, w_in_t, row0=0, width=c_ga, tm=2048, tn=1024, name="inproj_a")
    proj_s = _inproj(h, w_in_t, row0=c_sb, width=4 * d, tm=2048, tn=1024, name="inproj_s")

    o_a = _gla(proj_a, ga, gla_a_w2, gla_a_b, gla_head_norm_g, batch=batch, seq=seq,
               heads=GLA_HEADS, dk=dk, dv=dv, col_q=0, col_k=dk_total, col_v=2 * dk_total,
               col_g=2 * dk_total + dv_total, cs=256, nh=4)
    proj_m = _inproj(h, w_in_t, row0=c_sb + 4 * d, width=(1 + N_BRANCHES) * d,
                     tm=2048, tn=1024, name="inproj_m")
    o_b = _sb(proj_s, batch=batch, seq=seq, heads=sb_heads, d=SB_HEAD_DIM,
              col_q=0, col_k=d, col_v=2 * d, col_g=3 * d, tq=256, nh=16)
    o_m = _memattn(proj_m, mkv, batch=batch, seq=seq, heads=MEM_HEADS, d=mem_d,
                   col_q=0, tq=256, nh=4)

    merged = _merge(o_a, o_b, o_m, w_proj_gla.astype(BF16), w_proj_sb.astype(BF16),
                    w_proj_mem.astype(BF16), proj_m, col_gates=d, tm=1024, tn=512)
    out = _out(merged, w_out.astype(BF16), norm_post_g, x2d, tm=512)
    return out.reshape(batch, seq, d)
```

```python
import functools
import math

import jax
import jax.numpy as jnp
from jax import lax
from jax.experimental import pallas as pl
from jax.experimental.pallas import tpu as pltpu

F32 = jnp.float32
BF16 = jnp.bfloat16

EPS = 1e-6
N_MEM = 256
GLA_HEADS = 4
GLA_RANK = 16
GLA_TAU = 16.0
SB_HEAD_DIM = 128
MEM_HEADS = 4
N_BRANCHES = 3

LANES = 128
SUBLANES = 8
VMEM_LIMIT = 56 * 1024 * 1024
SB_LOG2_CUTOFF = 90.0 * math.log2(math.e)
SB_EXP2_CLAMP = 126.0


def _params(sem):
    return pltpu.CompilerParams(dimension_semantics=sem, vmem_limit_bytes=VMEM_LIMIT)


def _split_bf16(x):
    hi = x.astype(BF16)
    lo = (x - hi.astype(F32)).astype(BF16)
    return hi, lo


def _dot(a, b):
    return jnp.dot(a, b, preferred_element_type=F32)


def _dot_nt(a, b):
    return lax.dot_general(a, b, (((1,), (1,)), ((), ())), preferred_element_type=F32)


def _dot_tn(a, b):
    return lax.dot_general(a, b, (((0,), (0,)), ((), ())), preferred_element_type=F32)


def _softplus(z):
    return jnp.maximum(z, 0.0) + jnp.log(1.0 + jnp.exp(-jnp.abs(z)))


def _sigmoid(z):
    return 1.0 / (1.0 + jnp.exp(-z))


def _prenorm_kernel(x_ref, g_ref, wga_ref, h_ref, ga_ref):
    x = x_ref[...]
    ms = jnp.mean(x * x, axis=-1, keepdims=True)
    h = (x * lax.rsqrt(ms + EPS) * g_ref[...]).astype(BF16)
    h_ref[...] = h
    ga_ref[...] = _dot_nt(h, wga_ref[...].astype(BF16))


def _prenorm(x2d, g, w_in_t, col_ga, tm):
    m, d = x2d.shape
    return pl.pallas_call(
        _prenorm_kernel,
        grid=(m // tm,),
        in_specs=[
            pl.BlockSpec((tm, d), lambda i: (i, 0)),
            pl.BlockSpec((1, d), lambda i: (0, 0)),
            pl.BlockSpec((LANES, d), lambda i: (col_ga // LANES, 0)),
        ],
        out_specs=[
            pl.BlockSpec((tm, d), lambda i: (i, 0)),
            pl.BlockSpec((tm, LANES), lambda i: (i, 0)),
        ],
        out_shape=[
            jax.ShapeDtypeStruct((m, d), BF16),
            jax.ShapeDtypeStruct((m, LANES), F32),
        ],
        compiler_params=_params(("parallel",)),
        name="prenorm",
    )(x2d, g.reshape(1, d), w_in_t)


CAST_ROWS = 128


def _inproj_kernel(a_ref, w_ref, o_ref, wb_ref):
    @pl.when(pl.program_id(1) == 0)
    def _():
        for r in range(0, wb_ref.shape[0], CAST_ROWS):
            rows = slice(r, r + CAST_ROWS)
            wb_ref[rows, :] = w_ref[rows, :].astype(BF16)

    o_ref[...] = _dot_nt(a_ref[...], wb_ref[...]).astype(o_ref.dtype)


def _inproj(a, w_t, *, skip_at, skip, tm, tn):
    m, k = a.shape
    width = w_t.shape[0] - skip
    assert skip_at % tn == 0 and skip % SUBLANES == 0 and width % tn == 0

    def w_rows(j, i):
        shift = jnp.where(j * tn >= skip_at, skip // SUBLANES, 0)
        return (j * (tn // SUBLANES) + shift) * SUBLANES, 0

    return pl.pallas_call(
        _inproj_kernel,
        grid=(width // tn, m // tm),
        in_specs=[
            pl.BlockSpec((tm, k), lambda j, i: (i, 0)),
            pl.BlockSpec((pl.Element(tn), pl.Element(k)), w_rows),
        ],
        out_specs=pl.BlockSpec((tm, tn), lambda j, i: (i, j)),
        out_shape=jax.ShapeDtypeStruct((m, width), BF16),
        scratch_shapes=[pltpu.VMEM((tn, k), BF16)],
        compiler_params=_params(("parallel", "arbitrary")),
        name="inproj",
    )(a, w_t)


def _memkv_kernel(m_ref, g_ref, w_ref, o_ref):
    x = m_ref[...]
    ms = jnp.mean(x * x, axis=-1, keepdims=True)
    h = (x * lax.rsqrt(ms + EPS) * g_ref[...]).astype(BF16)
    o_ref[...] = _dot(h, w_ref[...].astype(BF16)).astype(o_ref.dtype)


def _memkv(mem2d, g, w, tn):
    m, d = mem2d.shape
    n = w.shape[1]
    return pl.pallas_call(
        _memkv_kernel,
        grid=(n // tn,),
        in_specs=[
            pl.BlockSpec((m, d), lambda j: (0, 0)),
            pl.BlockSpec((1, d), lambda j: (0, 0)),
            pl.BlockSpec((d, tn), lambda j: (0, j)),
        ],
        out_specs=pl.BlockSpec((m, tn), lambda j: (0, j)),
        out_shape=jax.ShapeDtypeStruct((m, n), BF16),
        compiler_params=_params(("parallel",)),
        name="memkv",
    )(mem2d, g.reshape(1, d), w)


def _gla_kernel(q_ref, k_ref, v_ref, gg_ref, ga_ref, w2h_ref, w2l_ref, ab_ref, hn_ref,
                o_ref, st_ref, *, cs, nh, dk, dv, scale):
    @pl.when(pl.program_id(2) == 0)
    def _():
        st_ref[...] = jnp.zeros_like(st_ref)

    ga_hi, ga_lo = _split_bf16(ga_ref[...])
    w2_hi = w2h_ref[...]
    z = _dot(ga_hi, w2_hi) + _dot(ga_lo, w2_hi) + _dot(ga_hi, w2l_ref[...]) + ab_ref[...]
    log_a = -_softplus(-z) * (math.log2(math.e) / GLA_TAU)

    row = lax.broadcasted_iota(jnp.int32, (cs, cs), 0)
    col = lax.broadcasted_iota(jnp.int32, (cs, cs), 1)
    causal = col <= row
    tri = jnp.where(causal, 1.0, 0.0).astype(BF16)
    la_hi, la_lo = _split_bf16(log_a)
    b_all = _dot(tri, la_hi) + _dot(tri, la_lo)

    kcols = [slice(h * dk, (h + 1) * dk) for h in range(nh)]
    vcols = [slice(h * dv, (h + 1) * dv) for h in range(nh)]
    for h in range(nh):
        b = b_all[:, kcols[h]]
        b_last = b[cs - 1:cs, :]
        b_mid = b[cs // 2 - 1:cs // 2, :]
        q = q_ref[:, kcols[h]].astype(F32) * scale
        k = k_ref[:, kcols[h]].astype(F32)
        v = v_ref[:, vcols[h]]
        q_inter = (q * jnp.exp2(b)).astype(BF16)
        k_carry = (k * jnp.exp2(b_last - b)).astype(BF16)
        q_intra = (q * jnp.exp2(b - b_mid)).astype(BF16)
        k_intra = (k * jnp.exp2(b_mid - b)).astype(BF16)

        attn = jnp.where(causal, _dot_nt(q_intra, k_intra), 0.0).astype(BF16)
        st = st_ref[h]
        o = _dot_nt(q_inter, st.astype(BF16)) + _dot(attn, v)
        st_ref[h] = st * jnp.exp2(b_last) + _dot_tn(v, k_carry)

        ms = jnp.mean(o * o, axis=-1, keepdims=True)
        o = o * lax.rsqrt(ms + EPS) * hn_ref[...]
        gg = gg_ref[:, vcols[h]].astype(F32)
        o_ref[:, vcols[h]] = (o * (gg * _sigmoid(gg))).astype(o_ref.dtype)


def _gla(proj, ga, w2, a_b, hn_g, *, batch, seq, heads, dk, dv, col_q, col_k,
         col_v, col_g, cs, nh):
    nc = seq // cs
    wk, wv = nh * dk, nh * dv
    w2_hi, w2_lo = _split_bf16(jnp.pad(w2, ((0, LANES - w2.shape[0]), (0, 0))))
    kern = functools.partial(_gla_kernel, cs=cs, nh=nh, dk=dk, dv=dv, scale=dk ** -0.5)
    row = lambda b, c: b * nc + c
    return pl.pallas_call(
        kern,
        grid=(batch, heads // nh, nc),
        in_specs=[
            pl.BlockSpec((cs, wk), lambda b, h, c: (row(b, c), col_q // wk + h)),
            pl.BlockSpec((cs, wk), lambda b, h, c: (row(b, c), col_k // wk + h)),
            pl.BlockSpec((cs, wv), lambda b, h, c: (row(b, c), col_v // wv + h)),
            pl.BlockSpec((cs, wv), lambda b, h, c: (row(b, c), col_g // wv + h)),
            pl.BlockSpec((cs, LANES), lambda b, h, c: (row(b, c), 0)),
            pl.BlockSpec((LANES, wk), lambda b, h, c: (0, h)),
            pl.BlockSpec((LANES, wk), lambda b, h, c: (0, h)),
            pl.BlockSpec((1, wk), lambda b, h, c: (0, h)),
            pl.BlockSpec((1, dv), lambda b, h, c: (0, 0)),
        ],
        out_specs=pl.BlockSpec((cs, wv), lambda b, h, c: (row(b, c), h)),
        out_shape=jax.ShapeDtypeStruct((batch * seq, heads * dv), BF16),
        scratch_shapes=[pltpu.VMEM((nh, dv, dk), F32)],
        compiler_params=_params(("parallel", "parallel", "arbitrary")),
        name="gla",
    )(proj, proj, proj, proj, ga, w2_hi, w2_lo, a_b.reshape(1, -1), hn_g.reshape(1, -1))


def _sb_kernel(q_ref, k_ref, v_ref, sg_ref, o_ref, carry_ref, acc_ref, *, tq, nh, d, scale2):
    qi = pl.program_id(2)

    r = lax.broadcasted_iota(jnp.int32, (2 * LANES, 2 * LANES), 0) % LANES
    c = lax.broadcasted_iota(jnp.int32, (2 * LANES, 2 * LANES), 1)
    cum_w = jnp.where((c >= LANES) | (r > c), 1.0, 0.0).astype(BF16)
    groups = tq // LANES

    def sweep(key_start, mask, first=False):
        heads = [slice(h * d, (h + 1) * d) for h in range(nh)]
        u = [_dot_nt(q_ref[:, cols], k_ref[pl.ds(key_start, tq), cols]) * scale2
             for cols in heads]
        sp, after = [], []
        for h in range(nh):
            sp_h = jnp.maximum(
                jnp.log2(1.0 + jnp.exp2(jnp.minimum(u[h], SB_EXP2_CLAMP))), u[h])
            sp_m = sp_h if mask is None else jnp.where(mask, sp_h, 0.0)
            carry = jnp.zeros((tq, LANES), F32) if first else carry_ref[h]
            after_h = [None] * groups
            for g in reversed(range(groups)):
                hi, lo = _split_bf16(sp_m[:, g * LANES:(g + 1) * LANES])
                ct = _dot(jnp.concatenate([hi, lo], axis=1), cum_w)
                after_h[g] = ct[:, :LANES] + carry
                carry = carry + ct[:, LANES:]
            carry_ref[h] = carry
            sp.append(sp_h)
            after.append(jnp.concatenate(after_h, axis=1))
        for h in range(nh):
            w = jnp.exp2(u[h] - sp[h] - after[h])
            if mask is not None:
                w = jnp.where(mask, w, 0.0)
            pv = _dot(w.astype(BF16), v_ref[pl.ds(key_start, tq), heads[h]])
            acc_ref[h] = pv if first else acc_ref[h] + pv

    def least_carry():
        m = carry_ref[0]
        for h in range(1, nh):
            m = jnp.minimum(m, carry_ref[h])
        return jnp.min(m)

    t_idx = lax.broadcasted_iota(jnp.int32, (tq, tq), 0)
    s_idx = lax.broadcasted_iota(jnp.int32, (tq, tq), 1)
    sweep(pl.multiple_of(qi * tq, tq), s_idx < t_idx, first=True)

    def cond(state):
        blk, low = state
        return jnp.logical_and(blk >= 0, low < SB_LOG2_CUTOFF)

    def body(state):
        blk, _ = state
        sweep(pl.multiple_of(blk * tq, tq), None)
        return blk - 1, least_carry()

    lax.while_loop(cond, body, (qi - 1, least_carry()))

    for h in range(nh):
        cols = slice(h * d, (h + 1) * d)
        sg = sg_ref[:, cols].astype(F32)
        o_ref[:, cols] = (acc_ref[h] * (sg * _sigmoid(sg))).astype(o_ref.dtype)


def _sb(proj, *, batch, seq, heads, d, col_q, col_k, col_v, col_g, tq, nh):
    nq = seq // tq
    w = nh * d
    kern = functools.partial(_sb_kernel, tq=tq, nh=nh, d=d, scale2=math.log2(math.e) / math.sqrt(d))
    return pl.pallas_call(
        kern,
        grid=(batch, heads // nh, nq),
        in_specs=[
            pl.BlockSpec((tq, w), lambda b, h, i: (b * nq + i, col_q // w + h)),
            pl.BlockSpec((seq, w), lambda b, h, i: (b, col_k // w + h)),
            pl.BlockSpec((seq, w), lambda b, h, i: (b, col_v // w + h)),
            pl.BlockSpec((tq, w), lambda b, h, i: (b * nq + i, col_g // w + h)),
        ],
        out_specs=pl.BlockSpec((tq, w), lambda b, h, i: (b * nq + i, h)),
        out_shape=jax.ShapeDtypeStruct((batch * seq, heads * d), BF16),
        scratch_shapes=[pltpu.VMEM((nh, tq, LANES), F32), pltpu.VMEM((nh, tq, d), F32)],
        compiler_params=_params(("parallel", "parallel", "arbitrary")),
        name="stickbreak",
    )(proj, proj, proj, proj)


def _memattn_kernel(q_ref, k_ref, v_ref, o_ref, *, nh, d, scale):
    heads = [slice(h * d, (h + 1) * d) for h in range(nh)]
    scores = [_dot_nt(q_ref[:, cols], k_ref[:, cols]) * scale for cols in heads]
    for cols, s in zip(heads, scores):
        s = s - jnp.max(s, axis=-1, keepdims=True)
        p = jnp.exp(s)
        p = p / jnp.sum(p, axis=-1, keepdims=True)
        o_ref[:, cols] = _dot(p.astype(BF16), v_ref[:, cols]).astype(o_ref.dtype)


def _memattn(proj, mkv, *, batch, seq, heads, d, col_q, tq, nh):
    nq = seq // tq
    w = nh * d
    kern = functools.partial(_memattn_kernel, nh=nh, d=d, scale=1.0 / math.sqrt(d))
    return pl.pallas_call(
        kern,
        grid=(batch, heads // nh, nq),
        in_specs=[
            pl.BlockSpec((tq, w), lambda b, h, i: (b * nq + i, col_q // w + h)),
            pl.BlockSpec((N_MEM, w), lambda b, h, i: (b, h)),
            pl.BlockSpec((N_MEM, w), lambda b, h, i: (b, heads // nh + h)),
        ],
        out_specs=pl.BlockSpec((tq, w), lambda b, h, i: (b * nq + i, h)),
        out_shape=jax.ShapeDtypeStruct((batch * seq, heads * d), BF16),
        compiler_params=_params(("parallel", "parallel", "parallel")),
        name="memattn",
    )(proj, mkv, mkv)


def _merge_kernel(oa_ref, ob_ref, om_ref, wa_ref, wb_ref, wm_ref,
                  ga_ref, gb_ref, gm_ref, o_ref):
    acc = _sigmoid(ga_ref[...].astype(F32)) * _dot(oa_ref[...], wa_ref[...])
    acc += _sigmoid(gb_ref[...].astype(F32)) * _dot(ob_ref[...], wb_ref[...])
    acc += _sigmoid(gm_ref[...].astype(F32)) * _dot(om_ref[...], wm_ref[...])
    o_ref[...] = acc.astype(o_ref.dtype)


def _merge(o_a, o_b, o_m, w_a, w_b, w_m, proj, *, col_gates, tm, tn):
    m, d = o_a.shape
    gate_blk = col_gates // tn
    act = lambda: pl.BlockSpec((tm, d), lambda i, j: (i, 0))
    wgt = lambda: pl.BlockSpec((d, tn), lambda i, j: (0, j))
    gate = lambda n: pl.BlockSpec((tm, tn), lambda i, j: (i, gate_blk + n * (d // tn) + j))
    return pl.pallas_call(
        _merge_kernel,
        grid=(m // tm, d // tn),
        in_specs=[act(), act(), act(), wgt(), wgt(), wgt(), gate(0), gate(1), gate(2)],
        out_specs=pl.BlockSpec((tm, tn), lambda i, j: (i, j)),
        out_shape=jax.ShapeDtypeStruct((m, d), BF16),
        compiler_params=_params(("parallel", "parallel")),
        name="merge",
    )(o_a, o_b, o_m, w_a, w_b, w_m, proj, proj, proj)


def _out_kernel(m_ref, w_ref, g_ref, x_ref, o_ref):
    y = _dot(m_ref[...], w_ref[...])
    ms = jnp.mean(y * y, axis=-1, keepdims=True)
    o_ref[...] = x_ref[...] + y * lax.rsqrt(ms + EPS) * g_ref[...]


def _out(merged, w_out, g, x2d, tm):
    m, d = x2d.shape
    return pl.pallas_call(
        _out_kernel,
        grid=(m // tm,),
        in_specs=[
            pl.BlockSpec((tm, d), lambda i: (i, 0)),
            pl.BlockSpec((d, d), lambda i: (0, 0)),
            pl.BlockSpec((1, d), lambda i: (0, 0)),
            pl.BlockSpec((tm, d), lambda i: (i, 0)),
        ],
        out_specs=pl.BlockSpec((tm, d), lambda i: (i, 0)),
        out_shape=jax.ShapeDtypeStruct((m, d), F32),
        compiler_params=_params(("parallel",)),
        name="outproj",
    )(merged, w_out, g.reshape(1, d), x2d)


def kernel(x, mem, norm_pre_g, norm_post_g, norm_mem_g, w_in, gla_a_w2, gla_a_b,
           gla_head_norm_g, w_mem_kv, w_proj_gla, w_proj_sb, w_proj_mem, w_out):
    batch, seq, d = x.shape
    dk_total, dv_total = d // 2, d
    dk, dv = dk_total // GLA_HEADS, dv_total // GLA_HEADS
    sb_heads = d // SB_HEAD_DIM
    mem_d = d // MEM_HEADS

    c_ga = 2 * dk_total + 2 * dv_total

    x2d = x.reshape(batch * seq, d)
    mem2d = mem.reshape(batch * N_MEM, d)

    mkv = _memkv(mem2d, norm_mem_g, w_mem_kv, tn=1024)
    w_in_t = w_in.T
    h, ga = _prenorm(x2d, norm_pre_g, w_in_t, c_ga, tm=1024)
    proj = _inproj(h, w_in_t, skip_at=c_ga, skip=GLA_RANK, tm=2048, tn=1024)
    c_sb, c_mq, c_gates = c_ga, c_ga + 4 * d, c_ga + 5 * d

    o_a = _gla(proj, ga, gla_a_w2, gla_a_b, gla_head_norm_g, batch=batch, seq=seq,
               heads=GLA_HEADS, dk=dk, dv=dv, col_q=0, col_k=dk_total, col_v=2 * dk_total,
               col_g=2 * dk_total + dv_total, cs=256, nh=4)
    o_b = _sb(proj, batch=batch, seq=seq, heads=sb_heads, d=SB_HEAD_DIM, col_q=c_sb,
              col_k=c_sb + d, col_v=c_sb + 2 * d, col_g=c_sb + 3 * d, tq=256, nh=16)
    o_m = _memattn(proj, mkv, batch=batch, seq=seq, heads=MEM_HEADS, d=mem_d,
                   col_q=c_mq, tq=256, nh=4)

    merged = _merge(o_a, o_b, o_m, w_proj_gla.astype(BF16), w_proj_sb.astype(BF16),
                    w_proj_mem.astype(BF16), proj, col_gates=c_gates, tm=1024, tn=512)
    out = _out(merged, w_out.astype(BF16), norm_post_g, x2d, tm=512)
    return out.reshape(batch, seq, d)
```

```python
import functools
import math
from typing import NamedTuple

import jax
import jax.numpy as jnp
from jax import lax
from jax.experimental import pallas as pl
from jax.experimental.pallas import tpu as pltpu

F32 = jnp.float32
BF16 = jnp.bfloat16

EPS = 1e-6
N_MEM = 256
GLA_HEADS = 4
GLA_RANK = 16
GLA_TAU = 16.0
SB_HEAD_DIM = 128
MEM_HEADS = 4
N_BRANCHES = 3

LANES = 128
SUBLANES = 8
VMEM_LIMIT = 56 * 1024 * 1024
SB_LOG2_CUTOFF = 90.0 * math.log2(math.e)
SB_EXP2_CLAMP = 126.0


class Tiles(NamedTuple):
    prenorm_tm: int = 1024
    memkv_tn: int = 1024
    inproj_tm: int = 2048
    inproj_tn: int = 1024
    gla_chunk: int = 256
    sb_tq: int = 256
    sb_heads_per_step: int = 16
    memattn_tq: int = 512
    merge_tm: int = 1024
    merge_tn: int = 512
    out_tm: int = 512


TILES = Tiles()


def _params(sem):
    return pltpu.CompilerParams(dimension_semantics=sem, vmem_limit_bytes=VMEM_LIMIT)


def _split_bf16(x):
    hi = x.astype(BF16)
    lo = (x - hi.astype(F32)).astype(BF16)
    return hi, lo


def _dot(a, b):
    return jnp.dot(a, b, preferred_element_type=F32)


def _dot_nt(a, b):
    return lax.dot_general(a, b, (((1,), (1,)), ((), ())), preferred_element_type=F32)


def _dot_tn(a, b):
    return lax.dot_general(a, b, (((0,), (0,)), ((), ())), preferred_element_type=F32)


def _softplus(z):
    return jnp.maximum(z, 0.0) + jnp.log(1.0 + jnp.exp(-jnp.abs(z)))


def _sigmoid(z):
    return 1.0 / (1.0 + jnp.exp(-z))


def _prenorm_kernel(x_ref, g_ref, wga_ref, h_ref, ga_ref):
    x = x_ref[...]
    ms = jnp.mean(x * x, axis=-1, keepdims=True)
    h = (x * lax.rsqrt(ms + EPS) * g_ref[...]).astype(BF16)
    h_ref[...] = h
    ga_ref[...] = _dot_nt(h, wga_ref[...].astype(BF16))


def _prenorm(x2d, g, w_in_t, col_ga, tm):
    m, d = x2d.shape
    return pl.pallas_call(
        _prenorm_kernel,
        grid=(m // tm,),
        in_specs=[
            pl.BlockSpec((tm, d), lambda i: (i, 0)),
            pl.BlockSpec((1, d), lambda i: (0, 0)),
            pl.BlockSpec((LANES, d), lambda i: (col_ga // LANES, 0)),
        ],
        out_specs=[
            pl.BlockSpec((tm, d), lambda i: (i, 0)),
            pl.BlockSpec((tm, LANES), lambda i: (i, 0)),
        ],
        out_shape=[
            jax.ShapeDtypeStruct((m, d), BF16),
            jax.ShapeDtypeStruct((m, LANES), F32),
        ],
        compiler_params=_params(("parallel",)),
        name="prenorm",
    )(x2d, g.reshape(1, d), w_in_t)


CAST_ROWS = 128


def _inproj_kernel(a_ref, w_ref, o_ref, wb_ref):
    @pl.when(pl.program_id(1) == 0)
    def _():
        for r in range(0, wb_ref.shape[0], CAST_ROWS):
            rows = slice(r, r + CAST_ROWS)
            wb_ref[rows, :] = w_ref[rows, :].astype(BF16)

    o_ref[...] = _dot_nt(a_ref[...], wb_ref[...]).astype(o_ref.dtype)


def _inproj(a, w_t, *, skip_at, skip, tm, tn):
    m, k = a.shape
    width = w_t.shape[0] - skip
    assert skip_at % tn == 0 and skip % SUBLANES == 0 and width % tn == 0

    def w_rows(j, i):
        shift = jnp.where(j * tn >= skip_at, skip // SUBLANES, 0)
        return (j * (tn // SUBLANES) + shift) * SUBLANES, 0

    return pl.pallas_call(
        _inproj_kernel,
        grid=(width // tn, m // tm),
        in_specs=[
            pl.BlockSpec((tm, k), lambda j, i: (i, 0)),
            pl.BlockSpec((pl.Element(tn), pl.Element(k)), w_rows),
        ],
        out_specs=pl.BlockSpec((tm, tn), lambda j, i: (i, j)),
        out_shape=jax.ShapeDtypeStruct((m, width), BF16),
        scratch_shapes=[pltpu.VMEM((tn, k), BF16)],
        compiler_params=_params(("parallel", "arbitrary")),
        name="inproj",
    )(a, w_t)


def _memkv_kernel(m_ref, g_ref, w_ref, o_ref):
    x = m_ref[...]
    ms = jnp.mean(x * x, axis=-1, keepdims=True)
    h = (x * lax.rsqrt(ms + EPS) * g_ref[...]).astype(BF16)
    o_ref[...] = _dot(h, w_ref[...].astype(BF16)).astype(o_ref.dtype)


def _memkv(mem2d, g, w, tn):
    m, d = mem2d.shape
    n = w.shape[1]
    return pl.pallas_call(
        _memkv_kernel,
        grid=(n // tn,),
        in_specs=[
            pl.BlockSpec((m, d), lambda j: (0, 0)),
            pl.BlockSpec((1, d), lambda j: (0, 0)),
            pl.BlockSpec((d, tn), lambda j: (0, j)),
        ],
        out_specs=pl.BlockSpec((m, tn), lambda j: (0, j)),
        out_shape=jax.ShapeDtypeStruct((m, n), BF16),
        compiler_params=_params(("parallel",)),
        name="memkv",
    )(mem2d, g.reshape(1, d), w)


def _gla_kernel(q_ref, k_ref, v_ref, gg_ref, ga_ref, w2h_ref, w2l_ref, ab_ref, hn_ref,
                o_ref, st_ref, *, cs, nh, dk, dv, scale):
    @pl.when(pl.program_id(2) == 0)
    def _():
        st_ref[...] = jnp.zeros_like(st_ref)

    ga_hi, ga_lo = _split_bf16(ga_ref[...])
    w2_hi = w2h_ref[...]
    z = _dot(ga_hi, w2_hi) + _dot(ga_lo, w2_hi) + _dot(ga_hi, w2l_ref[...]) + ab_ref[...]
    log_a = -_softplus(-z) * (math.log2(math.e) / GLA_TAU)

    row = lax.broadcasted_iota(jnp.int32, (cs, cs), 0)
    col = lax.broadcasted_iota(jnp.int32, (cs, cs), 1)
    causal = col <= row
    tri = jnp.where(causal, 1.0, 0.0).astype(BF16)
    la_hi, la_lo = _split_bf16(log_a)
    b_all = _dot(tri, la_hi) + _dot(tri, la_lo)

    kcols = [slice(h * dk, (h + 1) * dk) for h in range(nh)]
    vcols = [slice(h * dv, (h + 1) * dv) for h in range(nh)]
    for h in range(nh):
        b = b_all[:, kcols[h]]
        b_last = b[cs - 1:cs, :]
        b_mid = b[cs // 2 - 1:cs // 2, :]
        q = q_ref[:, kcols[h]].astype(F32) * scale
        k = k_ref[:, kcols[h]].astype(F32)
        v = v_ref[:, vcols[h]]
        q_inter = (q * jnp.exp2(b)).astype(BF16)
        k_carry = (k * jnp.exp2(b_last - b)).astype(BF16)
        q_intra = (q * jnp.exp2(b - b_mid)).astype(BF16)
        k_intra = (k * jnp.exp2(b_mid - b)).astype(BF16)

        attn = jnp.where(causal, _dot_nt(q_intra, k_intra), 0.0).astype(BF16)
        st = st_ref[h]
        o = _dot_nt(q_inter, st.astype(BF16)) + _dot(attn, v)
        st_ref[h] = st * jnp.exp2(b_last) + _dot_tn(v, k_carry)

        ms = jnp.mean(o * o, axis=-1, keepdims=True)
        o = o * lax.rsqrt(ms + EPS) * hn_ref[...]
        gg = gg_ref[:, vcols[h]].astype(F32)
        o_ref[:, vcols[h]] = (o * (gg * _sigmoid(gg))).astype(o_ref.dtype)


def _gla(proj, ga, w2, a_b, hn_g, *, batch, seq, heads, dk, dv, col_q, col_k,
         col_v, col_g, cs, nh):
    nc = seq // cs
    wk, wv = nh * dk, nh * dv
    w2_hi, w2_lo = _split_bf16(jnp.pad(w2, ((0, LANES - w2.shape[0]), (0, 0))))
    kern = functools.partial(_gla_kernel, cs=cs, nh=nh, dk=dk, dv=dv, scale=dk ** -0.5)
    row = lambda b, c: b * nc + c
    return pl.pallas_call(
        kern,
        grid=(batch, heads // nh, nc),
        in_specs=[
            pl.BlockSpec((cs, wk), lambda b, h, c: (row(b, c), col_q // wk + h)),
            pl.BlockSpec((cs, wk), lambda b, h, c: (row(b, c), col_k // wk + h)),
            pl.BlockSpec((cs, wv), lambda b, h, c: (row(b, c), col_v // wv + h)),
            pl.BlockSpec((cs, wv), lambda b, h, c: (row(b, c), col_g // wv + h)),
            pl.BlockSpec((cs, LANES), lambda b, h, c: (row(b, c), 0)),
            pl.BlockSpec((LANES, wk), lambda b, h, c: (0, h)),
            pl.BlockSpec((LANES, wk), lambda b, h, c: (0, h)),
            pl.BlockSpec((1, wk), lambda b, h, c: (0, h)),
            pl.BlockSpec((1, dv), lambda b, h, c: (0, 0)),
        ],
        out_specs=pl.BlockSpec((cs, wv), lambda b, h, c: (row(b, c), h)),
        out_shape=jax.ShapeDtypeStruct((batch * seq, heads * dv), BF16),
        scratch_shapes=[pltpu.VMEM((nh, dv, dk), F32)],
        compiler_params=_params(("parallel", "parallel", "arbitrary")),
        name="gla",
    )(proj, proj, proj, proj, ga, w2_hi, w2_lo, a_b.reshape(1, -1), hn_g.reshape(1, -1))


def _sb_kernel(q_ref, k_ref, v_ref, sg_ref, o_ref, carry_ref, acc_ref, *, tq, nh, d, scale2):
    qi = pl.program_id(2)

    r = lax.broadcasted_iota(jnp.int32, (2 * LANES, 2 * LANES), 0) % LANES
    c = lax.broadcasted_iota(jnp.int32, (2 * LANES, 2 * LANES), 1)
    cum_w = jnp.where((c >= LANES) | (r > c), 1.0, 0.0).astype(BF16)
    groups = tq // LANES

    def sweep(key_start, mask, first=False):
        heads = [slice(h * d, (h + 1) * d) for h in range(nh)]
        u = [_dot_nt(q_ref[:, cols], k_ref[pl.ds(key_start, tq), cols]) * scale2
             for cols in heads]
        sp, after = [], []
        for h in range(nh):
            sp_h = jnp.maximum(
                jnp.log2(1.0 + jnp.exp2(jnp.minimum(u[h], SB_EXP2_CLAMP))), u[h])
            sp_m = sp_h if mask is None else jnp.where(mask, sp_h, 0.0)
            carry = jnp.zeros((tq, LANES), F32) if first else carry_ref[h]
            after_h = [None] * groups
            for g in reversed(range(groups)):
                hi, lo = _split_bf16(sp_m[:, g * LANES:(g + 1) * LANES])
                ct = _dot(jnp.concatenate([hi, lo], axis=1), cum_w)
                after_h[g] = ct[:, :LANES] + carry
                carry = carry + ct[:, LANES:]
            carry_ref[h] = carry
            sp.append(sp_h)
            after.append(jnp.concatenate(after_h, axis=1))
        for h in range(nh):
            w = jnp.exp2(u[h] - sp[h] - after[h])
            if mask is not None:
                w = jnp.where(mask, w, 0.0)
            pv = _dot(w.astype(BF16), v_ref[pl.ds(key_start, tq), heads[h]])
            acc_ref[h] = pv if first else acc_ref[h] + pv

    def least_carry():
        m = carry_ref[0]
        for h in range(1, nh):
            m = jnp.minimum(m, carry_ref[h])
        return jnp.min(m)

    t_idx = lax.broadcasted_iota(jnp.int32, (tq, tq), 0)
    s_idx = lax.broadcasted_iota(jnp.int32, (tq, tq), 1)
    sweep(pl.multiple_of(qi * tq, tq), s_idx < t_idx, first=True)

    def cond(state):
        blk, low = state
        return jnp.logical_and(blk >= 0, low < SB_LOG2_CUTOFF)

    def body(state):
        blk, _ = state
        sweep(pl.multiple_of(blk * tq, tq), None)
        return blk - 1, least_carry()

    lax.while_loop(cond, body, (qi - 1, least_carry()))

    for h in range(nh):
        cols = slice(h * d, (h + 1) * d)
        sg = sg_ref[:, cols].astype(F32)
        o_ref[:, cols] = (acc_ref[h] * (sg * _sigmoid(sg))).astype(o_ref.dtype)


def _sb(proj, *, batch, seq, heads, d, col_q, col_k, col_v, col_g, tq, nh):
    nq = seq // tq
    w = nh * d
    kern = functools.partial(_sb_kernel, tq=tq, nh=nh, d=d, scale2=math.log2(math.e) / math.sqrt(d))
    return pl.pallas_call(
        kern,
        grid=(batch, heads // nh, nq),
        in_specs=[
            pl.BlockSpec((tq, w), lambda b, h, i: (b * nq + i, col_q // w + h)),
            pl.BlockSpec((seq, w), lambda b, h, i: (b, col_k // w + h)),
            pl.BlockSpec((seq, w), lambda b, h, i: (b, col_v // w + h)),
            pl.BlockSpec((tq, w), lambda b, h, i: (b * nq + i, col_g // w + h)),
        ],
        out_specs=pl.BlockSpec((tq, w), lambda b, h, i: (b * nq + i, h)),
        out_shape=jax.ShapeDtypeStruct((batch * seq, heads * d), BF16),
        scratch_shapes=[pltpu.VMEM((nh, tq, LANES), F32), pltpu.VMEM((nh, tq, d), F32)],
        compiler_params=_params(("parallel", "parallel", "arbitrary")),
        name="stickbreak",
    )(proj, proj, proj, proj)


def _memattn_kernel(q_ref, k_ref, v_ref, o_ref, *, nh, d, scale):
    heads = [slice(h * d, (h + 1) * d) for h in range(nh)]
    scores = [_dot_nt(q_ref[:, cols], k_ref[:, cols]) * scale for cols in heads]
    for cols, s in zip(heads, scores):
        s = s - jnp.max(s, axis=-1, keepdims=True)
        p = jnp.exp(s)
        p = p / jnp.sum(p, axis=-1, keepdims=True)
        o_ref[:, cols] = _dot(p.astype(BF16), v_ref[:, cols]).astype(o_ref.dtype)


def _memattn(proj, mkv, *, batch, seq, heads, d, col_q, tq, nh):
    nq = seq // tq
    w = nh * d
    kern = functools.partial(_memattn_kernel, nh=nh, d=d, scale=1.0 / math.sqrt(d))
    return pl.pallas_call(
        kern,
        grid=(batch, heads // nh, nq),
        in_specs=[
            pl.BlockSpec((tq, w), lambda b, h, i: (b * nq + i, col_q // w + h)),
            pl.BlockSpec((N_MEM, w), lambda b, h, i: (b, h)),
            pl.BlockSpec((N_MEM, w), lambda b, h, i: (b, heads // nh + h)),
        ],
        out_specs=pl.BlockSpec((tq, w), lambda b, h, i: (b * nq + i, h)),
        out_shape=jax.ShapeDtypeStruct((batch * seq, heads * d), BF16),
        compiler_params=_params(("parallel", "parallel", "parallel")),
        name="memattn",
    )(proj, mkv, mkv)


def _merge_kernel(oa_ref, ob_ref, om_ref, wa_ref, wb_ref, wm_ref,
                  ga_ref, gb_ref, gm_ref, o_ref):
    acc = _sigmoid(ga_ref[...].astype(F32)) * _dot(oa_ref[...], wa_ref[...])
    acc += _sigmoid(gb_ref[...].astype(F32)) * _dot(ob_ref[...], wb_ref[...])
    acc += _sigmoid(gm_ref[...].astype(F32)) * _dot(om_ref[...], wm_ref[...])
    o_ref[...] = acc.astype(o_ref.dtype)


def _merge(o_a, o_b, o_m, w_a, w_b, w_m, proj, *, col_gates, tm, tn):
    m, d = o_a.shape
    gate_blk = col_gates // tn
    act = lambda: pl.BlockSpec((tm, d), lambda i, j: (i, 0))
    wgt = lambda: pl.BlockSpec((d, tn), lambda i, j: (0, j))
    gate = lambda n: pl.BlockSpec((tm, tn), lambda i, j: (i, gate_blk + n * (d // tn) + j))
    return pl.pallas_call(
        _merge_kernel,
        grid=(m // tm, d // tn),
        in_specs=[act(), act(), act(), wgt(), wgt(), wgt(), gate(0), gate(1), gate(2)],
        out_specs=pl.BlockSpec((tm, tn), lambda i, j: (i, j)),
        out_shape=jax.ShapeDtypeStruct((m, d), BF16),
        compiler_params=_params(("parallel", "parallel")),
        name="merge",
    )(o_a, o_b, o_m, w_a, w_b, w_m, proj, proj, proj)


def _out_kernel(m_ref, w_ref, g_ref, x_ref, o_ref):
    y = _dot(m_ref[...], w_ref[...])
    ms = jnp.mean(y * y, axis=-1, keepdims=True)
    o_ref[...] = x_ref[...] + y * lax.rsqrt(ms + EPS) * g_ref[...]


def _out(merged, w_out, g, x2d, tm):
    m, d = x2d.shape
    return pl.pallas_call(
        _out_kernel,
        grid=(m // tm,),
        in_specs=[
            pl.BlockSpec((tm, d), lambda i: (i, 0)),
            pl.BlockSpec((d, d), lambda i: (0, 0)),
            pl.BlockSpec((1, d), lambda i: (0, 0)),
            pl.BlockSpec((tm, d), lambda i: (i, 0)),
        ],
        out_specs=pl.BlockSpec((tm, d), lambda i: (i, 0)),
        out_shape=jax.ShapeDtypeStruct((m, d), F32),
        compiler_params=_params(("parallel",)),
        name="outproj",
    )(merged, w_out, g.reshape(1, d), x2d)


def kernel(x, mem, norm_pre_g, norm_post_g, norm_mem_g, w_in, gla_a_w2, gla_a_b,
           gla_head_norm_g, w_mem_kv, w_proj_gla, w_proj_sb, w_proj_mem, w_out):
    batch, seq, d = x.shape
    dk_total, dv_total = d // 2, d
    dk, dv = dk_total // GLA_HEADS, dv_total // GLA_HEADS
    sb_heads = d // SB_HEAD_DIM
    mem_d = d // MEM_HEADS

    c_ga = 2 * dk_total + 2 * dv_total

    x2d = x.reshape(batch * seq, d)
    mem2d = mem.reshape(batch * N_MEM, d)

    t = TILES
    mkv = _memkv(mem2d, norm_mem_g, w_mem_kv, tn=t.memkv_tn)
    w_in_t = w_in.T
    h, ga = _prenorm(x2d, norm_pre_g, w_in_t, c_ga, tm=t.prenorm_tm)
    proj = _inproj(h, w_in_t, skip_at=c_ga, skip=GLA_RANK, tm=t.inproj_tm, tn=t.inproj_tn)
    c_sb, c_mq, c_gates = c_ga, c_ga + 4 * d, c_ga + 5 * d

    o_a = _gla(proj, ga, gla_a_w2, gla_a_b, gla_head_norm_g, batch=batch, seq=seq,
               heads=GLA_HEADS, dk=dk, dv=dv, col_q=0, col_k=dk_total, col_v=2 * dk_total,
               col_g=2 * dk_total + dv_total, cs=t.gla_chunk, nh=GLA_HEADS)
    o_b = _sb(proj, batch=batch, seq=seq, heads=sb_heads, d=SB_HEAD_DIM, col_q=c_sb,
              col_k=c_sb + d, col_v=c_sb + 2 * d, col_g=c_sb + 3 * d, tq=t.sb_tq,
              nh=min(t.sb_heads_per_step, sb_heads))
    o_m = _memattn(proj, mkv, batch=batch, seq=seq, heads=MEM_HEADS, d=mem_d,
                   col_q=c_mq, tq=t.memattn_tq, nh=MEM_HEADS)

    merged = _merge(o_a, o_b, o_m, w_proj_gla.astype(BF16), w_proj_sb.astype(BF16),
                    w_proj_mem.astype(BF16), proj, col_gates=c_gates, tm=t.merge_tm,
                    tn=t.merge_tn)
    out = _out(merged, w_out.astype(BF16), norm_post_g, x2d, tm=t.out_tm)
    return out.reshape(batch, seq, d)
```

```python
import functools
import math
from typing import NamedTuple

import jax
import jax.numpy as jnp
from jax import lax
from jax.experimental import pallas as pl
from jax.experimental.pallas import tpu as pltpu

F32 = jnp.float32
BF16 = jnp.bfloat16

EPS = 1e-6
N_MEM = 256
GLA_HEADS = 4
GLA_RANK = 16
GLA_TAU = 16.0
SB_HEAD_DIM = 128
MEM_HEADS = 4
N_BRANCHES = 3

LANES = 128
SUBLANES = 8
VMEM_LIMIT = 56 * 1024 * 1024
SB_LOG2_CUTOFF = 90.0 * math.log2(math.e)
SB_EXP2_CLAMP = 126.0


class Tiles(NamedTuple):
    prenorm_tm: int = 1024
    memkv_tn: int = 1024
    inproj_tm: int = 2048
    inproj_tn: int = 1024
    gla_chunk: int = 256
    sb_tq: int = 256
    sb_heads_per_step: int = 16
    memattn_tq: int = 1024
    merge_tm: int = 1024
    merge_tn: int = 512
    out_tm: int = 512


TILES = Tiles()


def _params(sem):
    return pltpu.CompilerParams(dimension_semantics=sem, vmem_limit_bytes=VMEM_LIMIT)


def _split_bf16(x):
    hi = x.astype(BF16)
    lo = (x - hi.astype(F32)).astype(BF16)
    return hi, lo


def _dot(a, b):
    return jnp.dot(a, b, preferred_element_type=F32)


def _dot_nt(a, b):
    return lax.dot_general(a, b, (((1,), (1,)), ((), ())), preferred_element_type=F32)


def _dot_tn(a, b):
    return lax.dot_general(a, b, (((0,), (0,)), ((), ())), preferred_element_type=F32)


def _softplus(z):
    return jnp.maximum(z, 0.0) + jnp.log(1.0 + jnp.exp(-jnp.abs(z)))


def _sigmoid(z):
    return 1.0 / (1.0 + jnp.exp(-z))


def _prenorm_kernel(x_ref, g_ref, wga_ref, h_ref, ga_ref):
    x = x_ref[...]
    ms = jnp.mean(x * x, axis=-1, keepdims=True)
    h = (x * lax.rsqrt(ms + EPS) * g_ref[...]).astype(BF16)
    h_ref[...] = h
    ga_ref[...] = _dot_nt(h, wga_ref[...].astype(BF16))


def _prenorm(x2d, g, w_in_t, col_ga, tm):
    m, d = x2d.shape
    return pl.pallas_call(
        _prenorm_kernel,
        grid=(m // tm,),
        in_specs=[
            pl.BlockSpec((tm, d), lambda i: (i, 0)),
            pl.BlockSpec((1, d), lambda i: (0, 0)),
            pl.BlockSpec((LANES, d), lambda i: (col_ga // LANES, 0)),
        ],
        out_specs=[
            pl.BlockSpec((tm, d), lambda i: (i, 0)),
            pl.BlockSpec((tm, LANES), lambda i: (i, 0)),
        ],
        out_shape=[
            jax.ShapeDtypeStruct((m, d), BF16),
            jax.ShapeDtypeStruct((m, LANES), F32),
        ],
        compiler_params=_params(("parallel",)),
        name="prenorm",
    )(x2d, g.reshape(1, d), w_in_t)


CAST_ROWS = 128


def _inproj_kernel(a_ref, w_ref, o_ref, wb_ref):
    @pl.when(pl.program_id(1) == 0)
    def _():
        for r in range(0, wb_ref.shape[0], CAST_ROWS):
            rows = slice(r, r + CAST_ROWS)
            wb_ref[rows, :] = w_ref[rows, :].astype(BF16)

    o_ref[...] = _dot_nt(a_ref[...], wb_ref[...]).astype(o_ref.dtype)


def _inproj(a, w_t, *, skip_at, skip, tm, tn):
    m, k = a.shape
    width = w_t.shape[0] - skip
    assert skip_at % tn == 0 and skip % SUBLANES == 0 and width % tn == 0

    def w_rows(j, i):
        shift = jnp.where(j * tn >= skip_at, skip // SUBLANES, 0)
        return (j * (tn // SUBLANES) + shift) * SUBLANES, 0

    return pl.pallas_call(
        _inproj_kernel,
        grid=(width // tn, m // tm),
        in_specs=[
            pl.BlockSpec((tm, k), lambda j, i: (i, 0)),
            pl.BlockSpec((pl.Element(tn), pl.Element(k)), w_rows),
        ],
        out_specs=pl.BlockSpec((tm, tn), lambda j, i: (i, j)),
        out_shape=jax.ShapeDtypeStruct((m, width), BF16),
        scratch_shapes=[pltpu.VMEM((tn, k), BF16)],
        compiler_params=_params(("parallel", "arbitrary")),
        name="inproj",
    )(a, w_t)


def _memkv_kernel(m_ref, g_ref, w_ref, o_ref):
    x = m_ref[...]
    ms = jnp.mean(x * x, axis=-1, keepdims=True)
    h = (x * lax.rsqrt(ms + EPS) * g_ref[...]).astype(BF16)
    o_ref[...] = _dot(h, w_ref[...].astype(BF16)).astype(o_ref.dtype)


def _memkv(mem2d, g, w, tn):
    m, d = mem2d.shape
    n = w.shape[1]
    return pl.pallas_call(
        _memkv_kernel,
        grid=(n // tn,),
        in_specs=[
            pl.BlockSpec((m, d), lambda j: (0, 0)),
            pl.BlockSpec((1, d), lambda j: (0, 0)),
            pl.BlockSpec((d, tn), lambda j: (0, j)),
        ],
        out_specs=pl.BlockSpec((m, tn), lambda j: (0, j)),
        out_shape=jax.ShapeDtypeStruct((m, n), BF16),
        compiler_params=_params(("parallel",)),
        name="memkv",
    )(mem2d, g.reshape(1, d), w)


def _gla_kernel(q_ref, k_ref, v_ref, gg_ref, ga_ref, w2h_ref, w2l_ref, ab_ref, hn_ref,
                o_ref, st_ref, *, cs, nh, dk, dv, scale):
    @pl.when(pl.program_id(2) == 0)
    def _():
        st_ref[...] = jnp.zeros_like(st_ref)

    ga_hi, ga_lo = _split_bf16(ga_ref[...])
    w2_hi = w2h_ref[...]
    z = _dot(ga_hi, w2_hi) + _dot(ga_lo, w2_hi) + _dot(ga_hi, w2l_ref[...]) + ab_ref[...]
    log_a = -_softplus(-z) * (math.log2(math.e) / GLA_TAU)

    row = lax.broadcasted_iota(jnp.int32, (cs, cs), 0)
    col = lax.broadcasted_iota(jnp.int32, (cs, cs), 1)
    causal = col <= row
    tri = jnp.where(causal, 1.0, 0.0).astype(BF16)
    la_hi, la_lo = _split_bf16(log_a)
    b_all = _dot(tri, la_hi) + _dot(tri, la_lo)

    kcols = [slice(h * dk, (h + 1) * dk) for h in range(nh)]
    vcols = [slice(h * dv, (h + 1) * dv) for h in range(nh)]
    for h in range(nh):
        b = b_all[:, kcols[h]]
        b_last = b[cs - 1:cs, :]
        b_mid = b[cs // 2 - 1:cs // 2, :]
        q = q_ref[:, kcols[h]].astype(F32) * scale
        k = k_ref[:, kcols[h]].astype(F32)
        v = v_ref[:, vcols[h]]
        q_inter = (q * jnp.exp2(b)).astype(BF16)
        k_carry = (k * jnp.exp2(b_last - b)).astype(BF16)
        q_intra = (q * jnp.exp2(b - b_mid)).astype(BF16)
        k_intra = (k * jnp.exp2(b_mid - b)).astype(BF16)

        attn = jnp.where(causal, _dot_nt(q_intra, k_intra), 0.0).astype(BF16)
        st = st_ref[h]
        o = _dot_nt(q_inter, st.astype(BF16)) + _dot(attn, v)
        st_ref[h] = st * jnp.exp2(b_last) + _dot_tn(v, k_carry)

        ms = jnp.mean(o * o, axis=-1, keepdims=True)
        o = o * lax.rsqrt(ms + EPS) * hn_ref[...]
        gg = gg_ref[:, vcols[h]].astype(F32)
        o_ref[:, vcols[h]] = (o * (gg * _sigmoid(gg))).astype(o_ref.dtype)


def _gla(proj, ga, w2, a_b, hn_g, *, batch, seq, heads, dk, dv, col_q, col_k,
         col_v, col_g, cs, nh):
    nc = seq // cs
    wk, wv = nh * dk, nh * dv
    w2_hi, w2_lo = _split_bf16(jnp.pad(w2, ((0, LANES - w2.shape[0]), (0, 0))))
    kern = functools.partial(_gla_kernel, cs=cs, nh=nh, dk=dk, dv=dv, scale=dk ** -0.5)
    row = lambda b, c: b * nc + c
    return pl.pallas_call(
        kern,
        grid=(batch, heads // nh, nc),
        in_specs=[
            pl.BlockSpec((cs, wk), lambda b, h, c: (row(b, c), col_q // wk + h)),
            pl.BlockSpec((cs, wk), lambda b, h, c: (row(b, c), col_k // wk + h)),
            pl.BlockSpec((cs, wv), lambda b, h, c: (row(b, c), col_v // wv + h)),
            pl.BlockSpec((cs, wv), lambda b, h, c: (row(b, c), col_g // wv + h)),
            pl.BlockSpec((cs, LANES), lambda b, h, c: (row(b, c), 0)),
            pl.BlockSpec((LANES, wk), lambda b, h, c: (0, h)),
            pl.BlockSpec((LANES, wk), lambda b, h, c: (0, h)),
            pl.BlockSpec((1, wk), lambda b, h, c: (0, h)),
            pl.BlockSpec((1, dv), lambda b, h, c: (0, 0)),
        ],
        out_specs=pl.BlockSpec((cs, wv), lambda b, h, c: (row(b, c), h)),
        out_shape=jax.ShapeDtypeStruct((batch * seq, heads * dv), BF16),
        scratch_shapes=[pltpu.VMEM((nh, dv, dk), F32)],
        compiler_params=_params(("parallel", "parallel", "arbitrary")),
        name="gla",
    )(proj, proj, proj, proj, ga, w2_hi, w2_lo, a_b.reshape(1, -1), hn_g.reshape(1, -1))


def _sb_kernel(q_ref, k_ref, v_ref, sg_ref, o_ref, carry_ref, acc_ref, *, tq, nh, d, scale2):
    qi = pl.program_id(2)

    r = lax.broadcasted_iota(jnp.int32, (2 * LANES, 2 * LANES), 0) % LANES
    c = lax.broadcasted_iota(jnp.int32, (2 * LANES, 2 * LANES), 1)
    cum_w = jnp.where((c >= LANES) | (r > c), 1.0, 0.0).astype(BF16)
    groups = tq // LANES

    def sweep(key_start, mask, first=False):
        heads = [slice(h * d, (h + 1) * d) for h in range(nh)]
        u = [_dot_nt(q_ref[:, cols], k_ref[pl.ds(key_start, tq), cols]) * scale2
             for cols in heads]
        sp, after = [], []
        for h in range(nh):
            sp_h = jnp.maximum(
                jnp.log2(1.0 + jnp.exp2(jnp.minimum(u[h], SB_EXP2_CLAMP))), u[h])
            sp_m = sp_h if mask is None else jnp.where(mask, sp_h, 0.0)
            carry = jnp.zeros((tq, LANES), F32) if first else carry_ref[h]
            after_h = [None] * groups
            for g in reversed(range(groups)):
                hi, lo = _split_bf16(sp_m[:, g * LANES:(g + 1) * LANES])
                ct = _dot(jnp.concatenate([hi, lo], axis=1), cum_w)
                after_h[g] = ct[:, :LANES] + carry
                carry = carry + ct[:, LANES:]
            carry_ref[h] = carry
            sp.append(sp_h)
            after.append(jnp.concatenate(after_h, axis=1))
        for h in range(nh):
            w = jnp.exp2(u[h] - sp[h] - after[h])
            if mask is not None:
                w = jnp.where(mask, w, 0.0)
            pv = _dot(w.astype(BF16), v_ref[pl.ds(key_start, tq), heads[h]])
            acc_ref[h] = pv if first else acc_ref[h] + pv

    def least_carry():
        m = carry_ref[0]
        for h in range(1, nh):
            m = jnp.minimum(m, carry_ref[h])
        return jnp.min(m)

    t_idx = lax.broadcasted_iota(jnp.int32, (tq, tq), 0)
    s_idx = lax.broadcasted_iota(jnp.int32, (tq, tq), 1)
    sweep(pl.multiple_of(qi * tq, tq), s_idx < t_idx, first=True)

    def cond(state):
        blk, low = state
        return jnp.logical_and(blk >= 0, low < SB_LOG2_CUTOFF)

    def body(state):
        blk, _ = state
        sweep(pl.multiple_of(blk * tq, tq), None)
        return blk - 1, least_carry()

    lax.while_loop(cond, body, (qi - 1, least_carry()))

    for h in range(nh):
        cols = slice(h * d, (h + 1) * d)
        sg = sg_ref[:, cols].astype(F32)
        o_ref[:, cols] = (acc_ref[h] * (sg * _sigmoid(sg))).astype(o_ref.dtype)


def _sb(proj, *, batch, seq, heads, d, col_q, col_k, col_v, col_g, tq, nh):
    nq = seq // tq
    w = nh * d
    kern = functools.partial(_sb_kernel, tq=tq, nh=nh, d=d, scale2=math.log2(math.e) / math.sqrt(d))
    return pl.pallas_call(
        kern,
        grid=(batch, heads // nh, nq),
        in_specs=[
            pl.BlockSpec((tq, w), lambda b, h, i: (b * nq + i, col_q // w + h)),
            pl.BlockSpec((seq, w), lambda b, h, i: (b, col_k // w + h)),
            pl.BlockSpec((seq, w), lambda b, h, i: (b, col_v // w + h)),
            pl.BlockSpec((tq, w), lambda b, h, i: (b * nq + i, col_g // w + h)),
        ],
        out_specs=pl.BlockSpec((tq, w), lambda b, h, i: (b * nq + i, h)),
        out_shape=jax.ShapeDtypeStruct((batch * seq, heads * d), BF16),
        scratch_shapes=[pltpu.VMEM((nh, tq, LANES), F32), pltpu.VMEM((nh, tq, d), F32)],
        compiler_params=_params(("parallel", "parallel", "arbitrary")),
        name="stickbreak",
    )(proj, proj, proj, proj)


def _memattn_kernel(q_ref, k_ref, v_ref, o_ref, *, nh, d, scale):
    heads = [slice(h * d, (h + 1) * d) for h in range(nh)]
    scores = [_dot_nt(q_ref[:, cols], k_ref[:, cols]) * scale for cols in heads]
    for cols, s in zip(heads, scores):
        s = s - jnp.max(s, axis=-1, keepdims=True)
        p = jnp.exp(s)
        p = p / jnp.sum(p, axis=-1, keepdims=True)
        o_ref[:, cols] = _dot(p.astype(BF16), v_ref[:, cols]).astype(o_ref.dtype)


def _memattn(proj, mkv, *, batch, seq, heads, d, col_q, tq, nh):
    nq = seq // tq
    w = nh * d
    kern = functools.partial(_memattn_kernel, nh=nh, d=d, scale=1.0 / math.sqrt(d))
    return pl.pallas_call(
        kern,
        grid=(batch, heads // nh, nq),
        in_specs=[
            pl.BlockSpec((tq, w), lambda b, h, i: (b * nq + i, col_q // w + h)),
            pl.BlockSpec((N_MEM, w), lambda b, h, i: (b, h)),
            pl.BlockSpec((N_MEM, w), lambda b, h, i: (b, heads // nh + h)),
        ],
        out_specs=pl.BlockSpec((tq, w), lambda b, h, i: (b * nq + i, h)),
        out_shape=jax.ShapeDtypeStruct((batch * seq, heads * d), BF16),
        compiler_params=_params(("parallel", "parallel", "parallel")),
        name="memattn",
    )(proj, mkv, mkv)


def _merge_kernel(oa_ref, ob_ref, om_ref, wa_ref, wb_ref, wm_ref,
                  ga_ref, gb_ref, gm_ref, o_ref):
    acc = _sigmoid(ga_ref[...].astype(F32)) * _dot(oa_ref[...], wa_ref[...])
    acc += _sigmoid(gb_ref[...].astype(F32)) * _dot(ob_ref[...], wb_ref[...])
    acc += _sigmoid(gm_ref[...].astype(F32)) * _dot(om_ref[...], wm_ref[...])
    o_ref[...] = acc.astype(o_ref.dtype)


def _merge(o_a, o_b, o_m, w_a, w_b, w_m, proj, *, col_gates, tm, tn):
    m, d = o_a.shape
    gate_blk = col_gates // tn
    act = lambda: pl.BlockSpec((tm, d), lambda i, j: (i, 0))
    wgt = lambda: pl.BlockSpec((d, tn), lambda i, j: (0, j))
    gate = lambda n: pl.BlockSpec((tm, tn), lambda i, j: (i, gate_blk + n * (d // tn) + j))
    return pl.pallas_call(
        _merge_kernel,
        grid=(m // tm, d // tn),
        in_specs=[act(), act(), act(), wgt(), wgt(), wgt(), gate(0), gate(1), gate(2)],
        out_specs=pl.BlockSpec((tm, tn), lambda i, j: (i, j)),
        out_shape=jax.ShapeDtypeStruct((m, d), BF16),
        compiler_params=_params(("parallel", "parallel")),
        name="merge",
    )(o_a, o_b, o_m, w_a, w_b, w_m, proj, proj, proj)


def _out_kernel(m_ref, w_ref, g_ref, x_ref, o_ref):
    y = _dot(m_ref[...], w_ref[...])
    ms = jnp.mean(y * y, axis=-1, keepdims=True)
    o_ref[...] = x_ref[...] + y * lax.rsqrt(ms + EPS) * g_ref[...]


def _out(merged, w_out, g, x2d, tm):
    m, d = x2d.shape
    return pl.pallas_call(
        _out_kernel,
        grid=(m // tm,),
        in_specs=[
            pl.BlockSpec((tm, d), lambda i: (i, 0)),
            pl.BlockSpec((d, d), lambda i: (0, 0)),
            pl.BlockSpec((1, d), lambda i: (0, 0)),
            pl.BlockSpec((tm, d), lambda i: (i, 0)),
        ],
        out_specs=pl.BlockSpec((tm, d), lambda i: (i, 0)),
        out_shape=jax.ShapeDtypeStruct((m, d), F32),
        compiler_params=_params(("parallel",)),
        name="outproj",
    )(merged, w_out, g.reshape(1, d), x2d)


def kernel(x, mem, norm_pre_g, norm_post_g, norm_mem_g, w_in, gla_a_w2, gla_a_b,
           gla_head_norm_g, w_mem_kv, w_proj_gla, w_proj_sb, w_proj_mem, w_out):
    batch, seq, d = x.shape
    dk_total, dv_total = d // 2, d
    dk, dv = dk_total // GLA_HEADS, dv_total // GLA_HEADS
    sb_heads = d // SB_HEAD_DIM
    mem_d = d // MEM_HEADS

    c_ga = 2 * dk_total + 2 * dv_total

    x2d = x.reshape(batch * seq, d)
    mem2d = mem.reshape(batch * N_MEM, d)

    t = TILES
    mkv = _memkv(mem2d, norm_mem_g, w_mem_kv, tn=t.memkv_tn)
    w_in_t = w_in.T
    h, ga = _prenorm(x2d, norm_pre_g, w_in_t, c_ga, tm=t.prenorm_tm)
    proj = _inproj(h, w_in_t, skip_at=c_ga, skip=GLA_RANK, tm=t.inproj_tm, tn=t.inproj_tn)
    c_sb, c_mq, c_gates = c_ga, c_ga + 4 * d, c_ga + 5 * d

    o_a = _gla(proj, ga, gla_a_w2, gla_a_b, gla_head_norm_g, batch=batch, seq=seq,
               heads=GLA_HEADS, dk=dk, dv=dv, col_q=0, col_k=dk_total, col_v=2 * dk_total,
               col_g=2 * dk_total + dv_total, cs=t.gla_chunk, nh=GLA_HEADS)
    o_b = _sb(proj, batch=batch, seq=seq, heads=sb_heads, d=SB_HEAD_DIM, col_q=c_sb,
              col_k=c_sb + d, col_v=c_sb + 2 * d, col_g=c_sb + 3 * d, tq=t.sb_tq,
              nh=min(t.sb_heads_per_step, sb_heads))
    o_m = _memattn(proj, mkv, batch=batch, seq=seq, heads=MEM_HEADS, d=mem_d,
                   col_q=c_mq, tq=t.memattn_tq, nh=MEM_HEADS)

    merged = _merge(o_a, o_b, o_m, w_proj_gla.astype(BF16), w_proj_sb.astype(BF16),
                    w_proj_mem.astype(BF16), proj, col_gates=c_gates, tm=t.merge_tm,
                    tn=t.merge_tn)
    out = _out(merged, w_out.astype(BF16), norm_post_g, x2d, tm=t.out_tm)
    return out.reshape(batch, seq, d)
```

```python
import functools
import math
from typing import NamedTuple

import jax
import jax.numpy as jnp
from jax import lax
from jax.experimental import pallas as pl
from jax.experimental.pallas import tpu as pltpu

F32 = jnp.float32
BF16 = jnp.bfloat16

EPS = 1e-6
N_MEM = 256
GLA_HEADS = 4
GLA_RANK = 16
GLA_TAU = 16.0
SB_HEAD_DIM = 128
MEM_HEADS = 4
N_BRANCHES = 3

LANES = 128
SUBLANES = 8
VMEM_LIMIT = 56 * 1024 * 1024
SB_LOG2_CUTOFF = 90.0 * math.log2(math.e)
SB_EXP2_CLAMP = 126.0


class Tiles(NamedTuple):
    prenorm_tm: int = 1024
    memkv_tn: int = 1024
    inproj_tm: int = 2048
    inproj_tn: int = 1024
    gla_chunk: int = 256
    sb_tq: int = 256
    sb_heads_per_step: int = 16
    memattn_tq: int = 1024
    merge_tm: int = 1024
    merge_tn: int = 512
    out_tm: int = 512


TILES = Tiles()


def _params(sem):
    return pltpu.CompilerParams(dimension_semantics=sem, vmem_limit_bytes=VMEM_LIMIT)


def _split_bf16(x):
    hi = x.astype(BF16)
    lo = (x - hi.astype(F32)).astype(BF16)
    return hi, lo


def _dot(a, b):
    return jnp.dot(a, b, preferred_element_type=F32)


def _dot_nt(a, b):
    return lax.dot_general(a, b, (((1,), (1,)), ((), ())), preferred_element_type=F32)


def _dot_tn(a, b):
    return lax.dot_general(a, b, (((0,), (0,)), ((), ())), preferred_element_type=F32)


def _softplus(z):
    return jnp.maximum(z, 0.0) + jnp.log(1.0 + jnp.exp(-jnp.abs(z)))


def _sigmoid(z):
    return 1.0 / (1.0 + jnp.exp(-z))


def _prenorm_kernel(x_ref, g_ref, wga_ref, h_ref, ga_ref):
    x = x_ref[...]
    ms = jnp.mean(x * x, axis=-1, keepdims=True)
    h = (x * lax.rsqrt(ms + EPS) * g_ref[...]).astype(BF16)
    h_ref[...] = h
    ga_ref[...] = _dot_nt(h, wga_ref[...].astype(BF16))


def _prenorm(x2d, g, w_in_t, col_ga, tm):
    m, d = x2d.shape
    return pl.pallas_call(
        _prenorm_kernel,
        grid=(m // tm,),
        in_specs=[
            pl.BlockSpec((tm, d), lambda i: (i, 0)),
            pl.BlockSpec((1, d), lambda i: (0, 0)),
            pl.BlockSpec((LANES, d), lambda i: (col_ga // LANES, 0)),
        ],
        out_specs=[
            pl.BlockSpec((tm, d), lambda i: (i, 0)),
            pl.BlockSpec((tm, LANES), lambda i: (i, 0)),
        ],
        out_shape=[
            jax.ShapeDtypeStruct((m, d), BF16),
            jax.ShapeDtypeStruct((m, LANES), F32),
        ],
        compiler_params=_params(("parallel",)),
        name="prenorm",
    )(x2d, g.reshape(1, d), w_in_t)


CAST_ROWS = 256


def _inproj_kernel(a_ref, w_ref, o_ref, wb_ref):
    first = pl.program_id(1) == 0

    @pl.when(first)
    def _():
        for r in range(0, wb_ref.shape[0], CAST_ROWS):
            rows = slice(r, r + CAST_ROWS)
            wb_ref[rows, :] = w_ref[rows, :].astype(BF16)
            o_ref[:, rows] = _dot_nt(a_ref[...], wb_ref[rows, :]).astype(o_ref.dtype)

    @pl.when(jnp.logical_not(first))
    def _():
        o_ref[...] = _dot_nt(a_ref[...], wb_ref[...]).astype(o_ref.dtype)


def _inproj(a, w_t, *, skip_at, skip, tm, tn):
    m, k = a.shape
    width = w_t.shape[0] - skip
    assert skip_at % tn == 0 and skip % SUBLANES == 0 and width % tn == 0

    def w_rows(j, i):
        shift = jnp.where(j * tn >= skip_at, skip // SUBLANES, 0)
        return (j * (tn // SUBLANES) + shift) * SUBLANES, 0

    return pl.pallas_call(
        _inproj_kernel,
        grid=(width // tn, m // tm),
        in_specs=[
            pl.BlockSpec((tm, k), lambda j, i: (i, 0)),
            pl.BlockSpec((pl.Element(tn), pl.Element(k)), w_rows),
        ],
        out_specs=pl.BlockSpec((tm, tn), lambda j, i: (i, j)),
        out_shape=jax.ShapeDtypeStruct((m, width), BF16),
        scratch_shapes=[pltpu.VMEM((tn, k), BF16)],
        compiler_params=_params(("parallel", "arbitrary")),
        name="inproj",
    )(a, w_t)


def _memkv_kernel(m_ref, g_ref, w_ref, o_ref):
    x = m_ref[...]
    ms = jnp.mean(x * x, axis=-1, keepdims=True)
    h = (x * lax.rsqrt(ms + EPS) * g_ref[...]).astype(BF16)
    o_ref[...] = _dot(h, w_ref[...].astype(BF16)).astype(o_ref.dtype)


def _memkv(mem2d, g, w, tn):
    m, d = mem2d.shape
    n = w.shape[1]
    return pl.pallas_call(
        _memkv_kernel,
        grid=(n // tn,),
        in_specs=[
            pl.BlockSpec((m, d), lambda j: (0, 0)),
            pl.BlockSpec((1, d), lambda j: (0, 0)),
            pl.BlockSpec((d, tn), lambda j: (0, j)),
        ],
        out_specs=pl.BlockSpec((m, tn), lambda j: (0, j)),
        out_shape=jax.ShapeDtypeStruct((m, n), BF16),
        compiler_params=_params(("parallel",)),
        name="memkv",
    )(mem2d, g.reshape(1, d), w)


def _gla_kernel(q_ref, k_ref, v_ref, gg_ref, ga_ref, w2h_ref, w2l_ref, ab_ref, hn_ref,
                o_ref, st_ref, *, cs, nh, dk, dv, scale):
    @pl.when(pl.program_id(2) == 0)
    def _():
        st_ref[...] = jnp.zeros_like(st_ref)

    ga_hi, ga_lo = _split_bf16(ga_ref[...])
    w2_hi = w2h_ref[...]
    z = _dot(ga_hi, w2_hi) + _dot(ga_lo, w2_hi) + _dot(ga_hi, w2l_ref[...]) + ab_ref[...]
    log_a = -_softplus(-z) * (math.log2(math.e) / GLA_TAU)

    row = lax.broadcasted_iota(jnp.int32, (cs, cs), 0)
    col = lax.broadcasted_iota(jnp.int32, (cs, cs), 1)
    causal = col <= row
    tri = jnp.where(causal, 1.0, 0.0).astype(BF16)
    la_hi, la_lo = _split_bf16(log_a)
    b_all = _dot(tri, la_hi) + _dot(tri, la_lo)

    kcols = [slice(h * dk, (h + 1) * dk) for h in range(nh)]
    vcols = [slice(h * dv, (h + 1) * dv) for h in range(nh)]
    for h in range(nh):
        b = b_all[:, kcols[h]]
        b_last = b[cs - 1:cs, :]
        b_mid = b[cs // 2 - 1:cs // 2, :]
        q = q_ref[:, kcols[h]].astype(F32) * scale
        k = k_ref[:, kcols[h]].astype(F32)
        v = v_ref[:, vcols[h]]
        q_inter = (q * jnp.exp2(b)).astype(BF16)
        k_carry = (k * jnp.exp2(b_last - b)).astype(BF16)
        q_intra = (q * jnp.exp2(b - b_mid)).astype(BF16)
        k_intra = (k * jnp.exp2(b_mid - b)).astype(BF16)

        attn = jnp.where(causal, _dot_nt(q_intra, k_intra), 0.0).astype(BF16)
        st = st_ref[h]
        o = _dot_nt(q_inter, st.astype(BF16)) + _dot(attn, v)
        st_ref[h] = st * jnp.exp2(b_last) + _dot_tn(v, k_carry)

        ms = jnp.mean(o * o, axis=-1, keepdims=True)
        o = o * lax.rsqrt(ms + EPS) * hn_ref[...]
        gg = gg_ref[:, vcols[h]].astype(F32)
        o_ref[:, vcols[h]] = (o * (gg * _sigmoid(gg))).astype(o_ref.dtype)


def _gla(proj, ga, w2, a_b, hn_g, *, batch, seq, heads, dk, dv, col_q, col_k,
         col_v, col_g, cs, nh):
    nc = seq // cs
    wk, wv = nh * dk, nh * dv
    w2_hi, w2_lo = _split_bf16(jnp.pad(w2, ((0, LANES - w2.shape[0]), (0, 0))))
    kern = functools.partial(_gla_kernel, cs=cs, nh=nh, dk=dk, dv=dv, scale=dk ** -0.5)
    row = lambda b, c: b * nc + c
    return pl.pallas_call(
        kern,
        grid=(batch, heads // nh, nc),
        in_specs=[
            pl.BlockSpec((cs, wk), lambda b, h, c: (row(b, c), col_q // wk + h)),
            pl.BlockSpec((cs, wk), lambda b, h, c: (row(b, c), col_k // wk + h)),
            pl.BlockSpec((cs, wv), lambda b, h, c: (row(b, c), col_v // wv + h)),
            pl.BlockSpec((cs, wv), lambda b, h, c: (row(b, c), col_g // wv + h)),
            pl.BlockSpec((cs, LANES), lambda b, h, c: (row(b, c), 0)),
            pl.BlockSpec((LANES, wk), lambda b, h, c: (0, h)),
            pl.BlockSpec((LANES, wk), lambda b, h, c: (0, h)),
            pl.BlockSpec((1, wk), lambda b, h, c: (0, h)),
            pl.BlockSpec((1, dv), lambda b, h, c: (0, 0)),
        ],
        out_specs=pl.BlockSpec((cs, wv), lambda b, h, c: (row(b, c), h)),
        out_shape=jax.ShapeDtypeStruct((batch * seq, heads * dv), BF16),
        scratch_shapes=[pltpu.VMEM((nh, dv, dk), F32)],
        compiler_params=_params(("parallel", "parallel", "arbitrary")),
        name="gla",
    )(proj, proj, proj, proj, ga, w2_hi, w2_lo, a_b.reshape(1, -1), hn_g.reshape(1, -1))


def _sb_kernel(q_ref, k_ref, v_ref, sg_ref, o_ref, carry_ref, acc_ref, *, tq, nh, d, scale2):
    qi = pl.program_id(2)

    r = lax.broadcasted_iota(jnp.int32, (2 * LANES, 2 * LANES), 0) % LANES
    c = lax.broadcasted_iota(jnp.int32, (2 * LANES, 2 * LANES), 1)
    cum_w = jnp.where((c >= LANES) | (r > c), 1.0, 0.0).astype(BF16)
    groups = tq // LANES

    def sweep(key_start, mask, first=False):
        heads = [slice(h * d, (h + 1) * d) for h in range(nh)]
        u = [_dot_nt(q_ref[:, cols], k_ref[pl.ds(key_start, tq), cols]) * scale2
             for cols in heads]
        sp, after = [], []
        for h in range(nh):
            sp_h = jnp.maximum(
                jnp.log2(1.0 + jnp.exp2(jnp.minimum(u[h], SB_EXP2_CLAMP))), u[h])
            sp_m = sp_h if mask is None else jnp.where(mask, sp_h, 0.0)
            carry = jnp.zeros((tq, LANES), F32) if first else carry_ref[h]
            after_h = [None] * groups
            for g in reversed(range(groups)):
                hi, lo = _split_bf16(sp_m[:, g * LANES:(g + 1) * LANES])
                ct = _dot(jnp.concatenate([hi, lo], axis=1), cum_w)
                after_h[g] = ct[:, :LANES] + carry
                carry = carry + ct[:, LANES:]
            carry_ref[h] = carry
            sp.append(sp_h)
            after.append(jnp.concatenate(after_h, axis=1))
        for h in range(nh):
            w = jnp.exp2(u[h] - sp[h] - after[h])
            if mask is not None:
                w = jnp.where(mask, w, 0.0)
            pv = _dot(w.astype(BF16), v_ref[pl.ds(key_start, tq), heads[h]])
            acc_ref[h] = pv if first else acc_ref[h] + pv

    def least_carry():
        m = carry_ref[0]
        for h in range(1, nh):
            m = jnp.minimum(m, carry_ref[h])
        return jnp.min(m)

    t_idx = lax.broadcasted_iota(jnp.int32, (tq, tq), 0)
    s_idx = lax.broadcasted_iota(jnp.int32, (tq, tq), 1)
    sweep(pl.multiple_of(qi * tq, tq), s_idx < t_idx, first=True)

    def cond(state):
        blk, low = state
        return jnp.logical_and(blk >= 0, low < SB_LOG2_CUTOFF)

    def body(state):
        blk, _ = state
        sweep(pl.multiple_of(blk * tq, tq), None)
        return blk - 1, least_carry()

    lax.while_loop(cond, body, (qi - 1, least_carry()))

    for h in range(nh):
        cols = slice(h * d, (h + 1) * d)
        sg = sg_ref[:, cols].astype(F32)
        o_ref[:, cols] = (acc_ref[h] * (sg * _sigmoid(sg))).astype(o_ref.dtype)


def _sb(proj, *, batch, seq, heads, d, col_q, col_k, col_v, col_g, tq, nh):
    nq = seq // tq
    w = nh * d
    kern = functools.partial(_sb_kernel, tq=tq, nh=nh, d=d, scale2=math.log2(math.e) / math.sqrt(d))
    return pl.pallas_call(
        kern,
        grid=(batch, heads // nh, nq),
        in_specs=[
            pl.BlockSpec((tq, w), lambda b, h, i: (b * nq + i, col_q // w + h)),
            pl.BlockSpec((seq, w), lambda b, h, i: (b, col_k // w + h)),
            pl.BlockSpec((seq, w), lambda b, h, i: (b, col_v // w + h)),
            pl.BlockSpec((tq, w), lambda b, h, i: (b * nq + i, col_g // w + h)),
        ],
        out_specs=pl.BlockSpec((tq, w), lambda b, h, i: (b * nq + i, h)),
        out_shape=jax.ShapeDtypeStruct((batch * seq, heads * d), BF16),
        scratch_shapes=[pltpu.VMEM((nh, tq, LANES), F32), pltpu.VMEM((nh, tq, d), F32)],
        compiler_params=_params(("parallel", "parallel", "arbitrary")),
        name="stickbreak",
    )(proj, proj, proj, proj)


def _memattn_kernel(q_ref, k_ref, v_ref, o_ref, *, nh, d, scale):
    heads = [slice(h * d, (h + 1) * d) for h in range(nh)]
    scores = [_dot_nt(q_ref[:, cols], k_ref[:, cols]) * scale for cols in heads]
    for cols, s in zip(heads, scores):
        s = s - jnp.max(s, axis=-1, keepdims=True)
        p = jnp.exp(s)
        p = p / jnp.sum(p, axis=-1, keepdims=True)
        o_ref[:, cols] = _dot(p.astype(BF16), v_ref[:, cols]).astype(o_ref.dtype)


def _memattn(proj, mkv, *, batch, seq, heads, d, col_q, tq, nh):
    nq = seq // tq
    w = nh * d
    kern = functools.partial(_memattn_kernel, nh=nh, d=d, scale=1.0 / math.sqrt(d))
    return pl.pallas_call(
        kern,
        grid=(batch, heads // nh, nq),
        in_specs=[
            pl.BlockSpec((tq, w), lambda b, h, i: (b * nq + i, col_q // w + h)),
            pl.BlockSpec((N_MEM, w), lambda b, h, i: (b, h)),
            pl.BlockSpec((N_MEM, w), lambda b, h, i: (b, heads // nh + h)),
        ],
        out_specs=pl.BlockSpec((tq, w), lambda b, h, i: (b * nq + i, h)),
        out_shape=jax.ShapeDtypeStruct((batch * seq, heads * d), BF16),
        compiler_params=_params(("parallel", "parallel", "parallel")),
        name="memattn",
    )(proj, mkv, mkv)


def _merge_kernel(oa_ref, ob_ref, om_ref, wa_ref, wb_ref, wm_ref,
                  ga_ref, gb_ref, gm_ref, o_ref):
    acc = _sigmoid(ga_ref[...].astype(F32)) * _dot(oa_ref[...], wa_ref[...])
    acc += _sigmoid(gb_ref[...].astype(F32)) * _dot(ob_ref[...], wb_ref[...])
    acc += _sigmoid(gm_ref[...].astype(F32)) * _dot(om_ref[...], wm_ref[...])
    o_ref[...] = acc.astype(o_ref.dtype)


def _merge(o_a, o_b, o_m, w_a, w_b, w_m, proj, *, col_gates, tm, tn):
    m, d = o_a.shape
    gate_blk = col_gates // tn
    act = lambda: pl.BlockSpec((tm, d), lambda i, j: (i, 0))
    wgt = lambda: pl.BlockSpec((d, tn), lambda i, j: (0, j))
    gate = lambda n: pl.BlockSpec((tm, tn), lambda i, j: (i, gate_blk + n * (d // tn) + j))
    return pl.pallas_call(
        _merge_kernel,
        grid=(m // tm, d // tn),
        in_specs=[act(), act(), act(), wgt(), wgt(), wgt(), gate(0), gate(1), gate(2)],
        out_specs=pl.BlockSpec((tm, tn), lambda i, j: (i, j)),
        out_shape=jax.ShapeDtypeStruct((m, d), BF16),
        compiler_params=_params(("parallel", "parallel")),
        name="merge",
    )(o_a, o_b, o_m, w_a, w_b, w_m, proj, proj, proj)


def _out_kernel(m_ref, w_ref, g_ref, x_ref, o_ref):
    y = _dot(m_ref[...], w_ref[...])
    ms = jnp.mean(y * y, axis=-1, keepdims=True)
    o_ref[...] = x_ref[...] + y * lax.rsqrt(ms + EPS) * g_ref[...]


def _out(merged, w_out, g, x2d, tm):
    m, d = x2d.shape
    return pl.pallas_call(
        _out_kernel,
        grid=(m // tm,),
        in_specs=[
            pl.BlockSpec((tm, d), lambda i: (i, 0)),
            pl.BlockSpec((d, d), lambda i: (0, 0)),
            pl.BlockSpec((1, d), lambda i: (0, 0)),
            pl.BlockSpec((tm, d), lambda i: (i, 0)),
        ],
        out_specs=pl.BlockSpec((tm, d), lambda i: (i, 0)),
        out_shape=jax.ShapeDtypeStruct((m, d), F32),
        compiler_params=_params(("parallel",)),
        name="outproj",
    )(merged, w_out, g.reshape(1, d), x2d)


def kernel(x, mem, norm_pre_g, norm_post_g, norm_mem_g, w_in, gla_a_w2, gla_a_b,
           gla_head_norm_g, w_mem_kv, w_proj_gla, w_proj_sb, w_proj_mem, w_out):
    batch, seq, d = x.shape
    dk_total, dv_total = d // 2, d
    dk, dv = dk_total // GLA_HEADS, dv_total // GLA_HEADS
    sb_heads = d // SB_HEAD_DIM
    mem_d = d // MEM_HEADS

    c_ga = 2 * dk_total + 2 * dv_total

    x2d = x.reshape(batch * seq, d)
    mem2d = mem.reshape(batch * N_MEM, d)

    t = TILES
    mkv = _memkv(mem2d, norm_mem_g, w_mem_kv, tn=t.memkv_tn)
    w_in_t = w_in.T
    h, ga = _prenorm(x2d, norm_pre_g, w_in_t, c_ga, tm=t.prenorm_tm)
    proj = _inproj(h, w_in_t, skip_at=c_ga, skip=GLA_RANK, tm=t.inproj_tm, tn=t.inproj_tn)
    c_sb, c_mq, c_gates = c_ga, c_ga + 4 * d, c_ga + 5 * d

    o_a = _gla(proj, ga, gla_a_w2, gla_a_b, gla_head_norm_g, batch=batch, seq=seq,
               heads=GLA_HEADS, dk=dk, dv=dv, col_q=0, col_k=dk_total, col_v=2 * dk_total,
               col_g=2 * dk_total + dv_total, cs=t.gla_chunk, nh=GLA_HEADS)
    o_b = _sb(proj, batch=batch, seq=seq, heads=sb_heads, d=SB_HEAD_DIM, col_q=c_sb,
              col_k=c_sb + d, col_v=c_sb + 2 * d, col_g=c_sb + 3 * d, tq=t.sb_tq,
              nh=min(t.sb_heads_per_step, sb_heads))
    o_m = _memattn(proj, mkv, batch=batch, seq=seq, heads=MEM_HEADS, d=mem_d,
                   col_q=c_mq, tq=t.memattn_tq, nh=MEM_HEADS)

    merged = _merge(o_a, o_b, o_m, w_proj_gla.astype(BF16), w_proj_sb.astype(BF16),
                    w_proj_mem.astype(BF16), proj, col_gates=c_gates, tm=t.merge_tm,
                    tn=t.merge_tn)
    out = _out(merged, w_out.astype(BF16), norm_post_g, x2d, tm=t.out_tm)
    return out.reshape(batch, seq, d)
```
